```python
import math
import jax, jax.numpy as jnp
from jax import lax
import numpy as np

D_MODEL = 1024
BATCH = 4
SEQ = 8192
DEPTH = 4

N_A_LAYERS = DEPTH // 2
N_B_LAYERS = DEPTH - N_A_LAYERS
RET_HEADS = 8
RET_QK_DIM = D_MODEL // RET_HEADS
RET_V_DIM = 2 * RET_QK_DIM
RET_V_WIDTH = RET_HEADS * RET_V_DIM
RET_CHUNK = 128
RET_ROPE_THETA = 10000.0
DIFF_HEAD_DIM = 64
DIFF_HEADS = D_MODEL // (2 * DIFF_HEAD_DIM)
DIFF_QK_WIDTH = DIFF_HEADS * 2 * DIFF_HEAD_DIM
DIFF_V_DIM = 2 * DIFF_HEAD_DIM
DIFF_V_WIDTH = DIFF_HEADS * DIFF_V_DIM
ROPE_THETA = 500000.0
ROPE_DIM = DIFF_HEAD_DIM // 4
Q_BLOCK = 128
NEG_BIG = -1e30
N_EXPERTS = 32
TOP_K = 4
D_FF = D_MODEL
SWIGLU_ALPHA = 1.702
SWIGLU_LIMIT = 7.0
MOE_BLOCK = 256
DN_ALPHA = (2 * DEPTH) ** 0.25
DN_BETA = (8 * DEPTH) ** -0.25
LN_EPS = 1e-5

kernel_name = "yoco_retention_diffattn_moe_deepnorm"


def layer_norm(x, g, b):
    xf = x.astype(jnp.float32)
    mu = jnp.mean(xf, -1, keepdims=True)
    var = jnp.mean(jnp.square(xf - mu), -1, keepdims=True)
    return ((xf - mu) * lax.rsqrt(var + LN_EPS) * g + b).astype(x.dtype)


def apply_rope(x, inv_freq):
    s = x.shape[1]
    rot = 2 * inv_freq.shape[0]
    ang = jnp.arange(s, dtype=jnp.float32)[:, None] * inv_freq[None, :]
    shape = (s,) + (1,) * (x.ndim - 3) + (rot // 2,)
    cos = jnp.cos(ang).reshape(shape)
    sin = jnp.sin(ang).reshape(shape)
    xr = x[..., :rot].astype(jnp.float32)
    x1, x2 = xr[..., : rot // 2], xr[..., rot // 2:]
    rotated = jnp.concatenate([x1 * cos - x2 * sin, x2 * cos + x1 * sin], -1).astype(x.dtype)
    return jnp.concatenate([rotated, x[..., rot:]], -1)


def partial_inv_freq():
    return 1.0 / (ROPE_THETA ** (jnp.arange(0, ROPE_DIM, 2, dtype=jnp.float32) / ROPE_DIM))


def retention(x, w_in, w_out):
    b, s, _ = x.shape
    h, dk, dv, c = RET_HEADS, RET_QK_DIM, RET_V_DIM, RET_CHUNK
    n = s // c
    proj = x @ w_in
    q, k, v, g = jnp.split(proj, [D_MODEL, 2 * D_MODEL, 2 * D_MODEL + RET_V_WIDTH], -1)
    inv_freq = 1.0 / (RET_ROPE_THETA ** jnp.linspace(0.0, 1.0, dk // 2, dtype=jnp.float32))
    q = apply_rope(q.reshape(b, s, h, dk), inv_freq)
    k = apply_rope(k.reshape(b, s, h, dk), inv_freq) * (dk ** -0.5)
    v = v.reshape(b, s, h, dv)

    def chunks(t):
        return t.reshape(b, n, c, h, -1).transpose(0, 3, 1, 2, 4).astype(jnp.float32)

    qc, kc, vc = chunks(q), chunks(k), chunks(v)
    log_gamma = jnp.log1p(-jnp.exp2(-5.0 - jnp.arange(h, dtype=jnp.float32)))
    idx = jnp.arange(c, dtype=jnp.float32)
    rel = idx[:, None] - idx[None, :]
    dmask = jnp.where(rel >= 0, jnp.exp(jnp.maximum(rel, 0.0) * log_gamma[:, None, None]), 0.0)
    scores = jnp.einsum('bhncd,bhnmd->bhncm', qc, kc) * dmask[:, None]
    o_inner = jnp.einsum('bhncm,bhnme->bhnce', scores, vc)
    k_decay = jnp.exp((c - 1 - idx) * log_gamma[:, None])
    q_decay = jnp.exp((idx + 1) * log_gamma[:, None])
    chunk_decay = jnp.exp(c * log_gamma)[:, None, None]
    kv = jnp.einsum('bhnmd,bhnme->nbhde', kc * k_decay[:, None, :, None], vc)

    def step(state, inp):
        q_n, kv_n = inp
        cross = jnp.einsum('bhcd,bhde->bhce', q_n, state)
        return chunk_decay * state + kv_n, cross

    state0 = jnp.zeros((b, h, dk, dv), jnp.float32)
    _, cross = lax.scan(step, state0, (qc.transpose(2, 0, 1, 3, 4), kv))
    o = o_inner + cross.transpose(1, 2, 0, 3, 4) * q_decay[:, None, :, None]
    mu = jnp.mean(o, -1, keepdims=True)
    var = jnp.mean(jnp.square(o - mu), -1, keepdims=True)
    o = (o - mu) * lax.rsqrt(var + LN_EPS)
    o = o.transpose(0, 2, 3, 1, 4).reshape(b, s, RET_V_WIDTH).astype(x.dtype)
    return (o * jax.nn.silu(g)) @ w_out


def shared_kv(x, w_kv):
    b, s, _ = x.shape
    kv = x @ w_kv
    k, v = jnp.split(kv, [DIFF_QK_WIDTH], -1)
    k = apply_rope(k.reshape(b, s, DIFF_HEADS, 2, DIFF_HEAD_DIM), partial_inv_freq())
    v = v.reshape(b, s, DIFF_HEADS, DIFF_V_DIM)
    return k, v


def diff_attention(x, k, v, w_q, lam, subln_g, w_out, lambda_init):
    b, s, _ = x.shape
    q = apply_rope((x @ w_q).reshape(b, s, DIFF_HEADS, 2, DIFF_HEAD_DIM), partial_inv_freq())
    q = q * (DIFF_HEAD_DIM ** -0.5)
    lam_f = lam.astype(jnp.float32)
    lam_full = (jnp.exp(jnp.sum(lam_f[0] * lam_f[1])) - jnp.exp(jnp.sum(lam_f[2] * lam_f[3]))
                + lambda_init)
    nq = s // Q_BLOCK
    qb = q.reshape(b, nq, Q_BLOCK, DIFF_HEADS, 2, DIFF_HEAD_DIM).transpose(1, 0, 2, 3, 4, 5)
    key_pos = jnp.arange(s)

    def block(inp):
        q_blk, i = inp
        q_pos = i * Q_BLOCK + jnp.arange(Q_BLOCK)
        sc = jnp.einsum('bqhcd,bkhcd->bhcqk', q_blk, k).astype(jnp.float32)
        sc = jnp.where(key_pos[None, :] <= q_pos[:, None], sc, NEG_BIG)
        p = jax.nn.softmax(sc, axis=-1)
        a = p[:, :, 0] - lam_full * p[:, :, 1]
        return jnp.einsum('bhqk,bkhe->bqhe', a.astype(v.dtype), v)

    o = lax.map(block, (qb, jnp.arange(nq)))
    o = o.transpose(1, 0, 2, 3, 4).reshape(b, s, DIFF_HEADS, DIFF_V_DIM).astype(jnp.float32)
    o = o * lax.rsqrt(jnp.mean(o * o, -1, keepdims=True) + LN_EPS) * subln_g * (1.0 - lambda_init)
    return o.reshape(b, s, DIFF_V_WIDTH).astype(x.dtype) @ w_out


def moe(x, w_router, b_router, w_up, b_up, w_down, b_down):
    b, s, d = x.shape
    t = b * s
    xf = x.reshape(t, d)
    logits = (xf @ w_router + b_router).astype(jnp.float32)
    top_val, top_idx = lax.top_k(logits, TOP_K)
    gates = jax.nn.softmax(top_val, axis=-1)
    flat_e = top_idx.reshape(-1)
    flat_tok = jnp.arange(t * TOP_K, dtype=jnp.int32) // TOP_K
    flat_gate = gates.reshape(-1)
    order = jnp.argsort(flat_e)
    sorted_e = flat_e[order]
    counts = jnp.zeros((N_EXPERTS,), jnp.int32).at[flat_e].add(1)
    offsets = jnp.cumsum(counts) - counts
    padded = (counts + MOE_BLOCK - 1) // MOE_BLOCK * MOE_BLOCK
    padded_end = jnp.cumsum(padded)
    padded_off = padded_end - padded
    rank = jnp.arange(t * TOP_K, dtype=jnp.int32) - offsets[sorted_e]
    dest = padded_off[sorted_e] + rank
    n_rows = t * TOP_K + N_EXPERTS * MOE_BLOCK
    n_blocks = n_rows // MOE_BLOCK
    row_tok = jnp.full((n_rows,), t, jnp.int32).at[dest].set(flat_tok[order])
    row_gate = jnp.zeros((n_rows,), jnp.float32).at[dest].set(flat_gate[order])
    block_e = jnp.minimum(
        jnp.searchsorted(padded_end, jnp.arange(n_blocks, dtype=jnp.int32) * MOE_BLOCK, side='right'),
        N_EXPERTS - 1)
    x_pad = jnp.concatenate([xf, jnp.zeros((1, d), xf.dtype)], 0)
    xs = x_pad[row_tok].reshape(n_blocks, MOE_BLOCK, d)

    def expert_block(inp):
        xb, e = inp
        hdn = xb @ w_up[e] + b_up[e]
        glu, lin = jnp.split(hdn, 2, -1)
        glu = jnp.minimum(glu, SWIGLU_LIMIT)
        lin = jnp.clip(lin, -SWIGLU_LIMIT, SWIGLU_LIMIT)
        hdn = glu * jax.nn.sigmoid(SWIGLU_ALPHA * glu) * (lin + 1.0)
        return hdn @ w_down[e] + b_down[e]

    ys = lax.map(expert_block, (xs, block_e)).reshape(n_rows, d)
    ys = ys * row_gate[:, None].astype(ys.dtype)
    out = jax.ops.segment_sum(ys, row_tok, num_segments=t + 1)[:t]
    return out.reshape(b, s, d)


def setup_inputs(seed: int = 0) -> dict:
    key = jax.random.key(seed)
    ks = jax.random.split(key, 24)
    f32 = jnp.float32

    def nrm(k, shape, scale):
        return jax.random.normal(k, shape, f32) * scale

    din = D_MODEL ** -0.5
    x = nrm(ks[0], (BATCH, SEQ, D_MODEL), 1.0)
    ret_w_in = jnp.concatenate([
        nrm(ks[1], (N_A_LAYERS, D_MODEL, 2 * D_MODEL), din),
        nrm(ks[2], (N_A_LAYERS, D_MODEL, RET_V_WIDTH), din * DN_BETA),
        nrm(ks[3], (N_A_LAYERS, D_MODEL, RET_V_WIDTH), din),
    ], -1)
    ret_w_out = nrm(ks[4], (N_A_LAYERS, RET_V_WIDTH, D_MODEL), RET_V_WIDTH ** -0.5 * DN_BETA)
    kv_w = jnp.concatenate([
        nrm(ks[5], (D_MODEL, DIFF_QK_WIDTH), din),
        nrm(ks[6], (D_MODEL, DIFF_V_WIDTH), din * DN_BETA),
    ], -1)
    diff_w_q = nrm(ks[7], (N_B_LAYERS, D_MODEL, DIFF_QK_WIDTH), din)
    diff_lambda = nrm(ks[8], (N_B_LAYERS, 4, DIFF_HEAD_DIM), 0.1)
    diff_subln_g = 1.0 + nrm(ks[9], (N_B_LAYERS, DIFF_V_DIM), 0.02)
    diff_w_out = nrm(ks[10], (N_B_LAYERS, DIFF_V_WIDTH, D_MODEL), DIFF_V_WIDTH ** -0.5 * DN_BETA)
    ln_attn_g = 1.0 + nrm(ks[11], (DEPTH, D_MODEL), 0.02)
    ln_attn_b = nrm(ks[12], (DEPTH, D_MODEL), 0.02)
    ln_ffn_g = 1.0 + nrm(ks[13], (DEPTH, D_MODEL), 0.02)
    ln_ffn_b = nrm(ks[14], (DEPTH, D_MODEL), 0.02)
    moe_w_router = nrm(ks[15], (DEPTH, D_MODEL, N_EXPERTS), din)
    moe_b_router = nrm(ks[16], (DEPTH, N_EXPERTS), 0.01)
    moe_w_up = nrm(ks[17], (DEPTH, N_EXPERTS, D_MODEL, 2 * D_FF), din * DN_BETA)
    moe_b_up = nrm(ks[18], (DEPTH, N_EXPERTS, 2 * D_FF), 0.01)
    moe_w_down = nrm(ks[19], (DEPTH, N_EXPERTS, D_FF, D_MODEL), D_FF ** -0.5 * DN_BETA)
    moe_b_down = nrm(ks[20], (DEPTH, N_EXPERTS, D_MODEL), 0.01)
    return {"x": x, "ret_w_in": ret_w_in, "ret_w_out": ret_w_out, "kv_w": kv_w,
            "diff_w_q": diff_w_q, "diff_lambda": diff_lambda, "diff_subln_g": diff_subln_g,
            "diff_w_out": diff_w_out, "ln_attn_g": ln_attn_g, "ln_attn_b": ln_attn_b,
            "ln_ffn_g": ln_ffn_g, "ln_ffn_b": ln_ffn_b, "moe_w_router": moe_w_router,
            "moe_b_router": moe_b_router, "moe_w_up": moe_w_up, "moe_b_up": moe_b_up,
            "moe_w_down": moe_w_down, "moe_b_down": moe_b_down}


def reference(x, ret_w_in, ret_w_out, kv_w, diff_w_q, diff_lambda, diff_subln_g, diff_w_out,
              ln_attn_g, ln_attn_b, ln_ffn_g, ln_ffn_b, moe_w_router, moe_b_router,
              moe_w_up, moe_b_up, moe_w_down, moe_b_down):
    h = x
    k_sh, v_sh = None, None
    for l in range(DEPTH):
        if l < N_A_LAYERS:
            mix = retention(h, ret_w_in[l], ret_w_out[l])
        else:
            j = l - N_A_LAYERS
            lambda_init = 0.8 - 0.6 * math.exp(-0.3 * l)
            mix = diff_attention(h, k_sh, v_sh, diff_w_q[j], diff_lambda[j], diff_subln_g[j],
                                 diff_w_out[j], lambda_init)
        h = layer_norm(DN_ALPHA * h + mix, ln_attn_g[l], ln_attn_b[l])
        ffn = moe(h, moe_w_router[l], moe_b_router[l], moe_w_up[l], moe_b_up[l],
                  moe_w_down[l], moe_b_down[l])
        h = layer_norm(DN_ALPHA * h + ffn, ln_ffn_g[l], ln_ffn_b[l])
        if l == N_A_LAYERS - 1:
            k_sh, v_sh = shared_kv(h, kv_w)
    return h
```

```python
import functools
import math

import jax
import jax.numpy as jnp
from jax import lax
from jax.experimental import pallas as pl
from jax.experimental.pallas import tpu as pltpu

F32 = jnp.float32
BF16 = jnp.bfloat16

D_MODEL = 1024
DEPTH = 4
N_A_LAYERS = DEPTH // 2
RET_HEADS = 8
RET_QK_DIM = D_MODEL // RET_HEADS
RET_V_DIM = 2 * RET_QK_DIM
RET_V_WIDTH = RET_HEADS * RET_V_DIM
RET_ROPE_THETA = 10000.0
DIFF_HEAD_DIM = 64
DIFF_HEADS = D_MODEL // (2 * DIFF_HEAD_DIM)
DIFF_V_DIM = 2 * DIFF_HEAD_DIM
ROPE_THETA = 500000.0
ROPE_DIM = DIFF_HEAD_DIM // 4
NEG_BIG = -1e30
N_EXPERTS = 32
TOP_K = 4
D_FF = D_MODEL
SWIGLU_ALPHA = 1.702
SWIGLU_LIMIT = 7.0
DN_ALPHA = (2 * DEPTH) ** 0.25
LN_EPS = 1e-5

LANES = 128
VMEM_LIMIT = 48 * 1024 * 1024

ROW_TILE = 512
RET_CHUNK = 256
ATTN_TILE = 512
MOE_ROWS = 512
ROUTER_PAD = LANES


def _cparams(*sem):
    return pltpu.CompilerParams(dimension_semantics=sem, vmem_limit_bytes=VMEM_LIMIT)


def _layer_norm(y, g, b):
    mu = jnp.mean(y, -1, keepdims=True)
    d = y - mu
    var = jnp.mean(d * d, -1, keepdims=True)
    return d * lax.rsqrt(var + LN_EPS) * g + b


def _proj_kernel(x_ref, w_ref, c_ref, s_ref, scale_ref, o_ref, *, rope):
    acc = jnp.dot(x_ref[...].astype(BF16), w_ref[...], preferred_element_type=F32)
    tn = acc.shape[1]
    if rope is None:
        o_ref[...] = acc.astype(o_ref.dtype)
        return
    c = c_ref[...]
    s = s_ref[...]
    lane = lax.broadcasted_iota(jnp.int32, (1, LANES), 1)
    for g in range(tn // LANES):
        sl = slice(g * LANES, (g + 1) * LANES)
        xg = acc[:, sl]
        if rope == "half":
            y = xg * c + pltpu.roll(xg, LANES // 2, 1) * s
        else:
            half = ROPE_DIM // 2
            up = pltpu.roll(xg, LANES - half, 1)
            dn = pltpu.roll(xg, half, 1)
            partner = jnp.where((lane & half) == 0, up, dn)
            y = jnp.where((lane & (DIFF_HEAD_DIM - 1)) < ROPE_DIM, xg * c + partner * s, xg)
        o_ref[:, sl] = (y * scale_ref[:, sl]).astype(o_ref.dtype)


def _proj(x, w, out_dtype, *, rope=None, cos=None, sin=None, scale=None, seq=None, tn=1024):
    t, k = x.shape
    n = w.shape[1]
    tm = min(ROW_TILE, t)
    tn = min(tn, n)
    if cos is None:
        cos = jnp.zeros((tm, LANES), F32)
        sin = cos
        pos_blocks = 1
    else:
        pos_blocks = seq // tm
    if scale is None:
        scale = jnp.ones((1, n), F32)
    return pl.pallas_call(
        functools.partial(_proj_kernel, rope=rope),
        grid=(n // tn, t // tm),
        in_specs=[
            pl.BlockSpec((tm, k), lambda j, i: (i, 0)),
            pl.BlockSpec((k, tn), lambda j, i: (0, j)),
            pl.BlockSpec((tm, LANES), lambda j, i: (i % pos_blocks, 0)),
            pl.BlockSpec((tm, LANES), lambda j, i: (i % pos_blocks, 0)),
            pl.BlockSpec((1, tn), lambda j, i: (0, j)),
        ],
        out_specs=pl.BlockSpec((tm, tn), lambda j, i: (i, j)),
        out_shape=jax.ShapeDtypeStruct((t, n), out_dtype),
        compiler_params=_cparams("parallel", "parallel"),
        name="proj",
    )(x, w, cos, sin, scale)


def _outproj_ln_kernel(a_ref, w_ref, h_ref, g_ref, b_ref, o_ref):
    mix = jnp.dot(a_ref[...], w_ref[...], preferred_element_type=F32)
    o_ref[...] = _layer_norm(DN_ALPHA * h_ref[...] + mix, g_ref[...], b_ref[...])


def _outproj_ln(a, w, h, g, b):
    t, k = a.shape
    d = w.shape[1]
    tm = min(ROW_TILE, t)
    return pl.pallas_call(
        _outproj_ln_kernel,
        grid=(t // tm,),
        in_specs=[
            pl.BlockSpec((tm, k), lambda i: (i, 0)),
            pl.BlockSpec((k, d), lambda i: (0, 0)),
            pl.BlockSpec((tm, d), lambda i: (i, 0)),
            pl.BlockSpec((1, d), lambda i: (0, 0)),
            pl.BlockSpec((1, d), lambda i: (0, 0)),
        ],
        out_specs=pl.BlockSpec((tm, d), lambda i: (i, 0)),
        out_shape=jax.ShapeDtypeStruct((t, d), F32),
        compiler_params=_cparams("parallel"),
        name="outproj_ln",
    )(a, w, h, g.reshape(1, d), b.reshape(1, d))


def _residual_ln_kernel(h_ref, f_ref, g_ref, b_ref, o_ref):
    o_ref[...] = _layer_norm(DN_ALPHA * h_ref[...] + f_ref[...], g_ref[...], b_ref[...])


def _residual_ln(h, f, g, b):
    t, d = h.shape
    tm = min(ROW_TILE, t)
    return pl.pallas_call(
        _residual_ln_kernel,
        grid=(t // tm,),
        in_specs=[
            pl.BlockSpec((tm, d), lambda i: (i, 0)),
            pl.BlockSpec((tm, d), lambda i: (i, 0)),
            pl.BlockSpec((1, d), lambda i: (0, 0)),
            pl.BlockSpec((1, d), lambda i: (0, 0)),
        ],
        out_specs=pl.BlockSpec((tm, d), lambda i: (i, 0)),
        out_shape=jax.ShapeDtypeStruct((t, d), F32),
        compiler_params=_cparams("parallel"),
        name="residual_ln",
    )(h, f, g.reshape(1, d), b.reshape(1, d))


def _retention_kernel(q_ref, k_ref, v_ref, g_ref, dm_ref, qd_ref, kd_ref, cd_ref, o_ref, state_ref):
    @pl.when(pl.program_id(2) == 0)
    def _():
        state_ref[...] = jnp.zeros_like(state_ref)

    q = q_ref[...]
    k = k_ref[...]
    v = v_ref[...]
    s = lax.dot_general(q, k, (((1,), (1,)), ((), ())), preferred_element_type=F32)
    s = s * dm_ref[0]
    o = jnp.dot(s.astype(BF16), v, preferred_element_type=F32)
    state = state_ref[...]
    cross = jnp.dot(q, state.astype(BF16), preferred_element_type=F32)
    o = o + cross * qd_ref[0]
    kdec = (k.astype(F32) * kd_ref[0]).astype(BF16)
    kv = lax.dot_general(kdec, v, (((0,), (0,)), ((), ())), preferred_element_type=F32)
    state_ref[...] = cd_ref[0] * state + kv
    mu = jnp.mean(o, -1, keepdims=True)
    d = o - mu
    var = jnp.mean(d * d, -1, keepdims=True)
    gate = g_ref[...]
    o_ref[...] = (d * lax.rsqrt(var + LN_EPS) * (gate * jax.nn.sigmoid(gate))).astype(o_ref.dtype)


def _retention_tables(c):
    h = RET_HEADS
    log_gamma = jnp.log1p(-jnp.exp2(-5.0 - jnp.arange(h, dtype=F32)))
    idx = jnp.arange(c, dtype=F32)
    rel = idx[:, None] - idx[None, :]
    dmask = jnp.where(rel >= 0, jnp.exp(jnp.maximum(rel, 0.0) * log_gamma[:, None, None]), 0.0)
    k_decay = jnp.exp((c - 1 - idx) * log_gamma[:, None])[:, :, None]
    q_decay = jnp.exp((idx + 1) * log_gamma[:, None])[:, :, None]
    chunk_decay = jnp.exp(c * log_gamma)[:, None, None]
    return dmask, q_decay, k_decay, chunk_decay


def _retention(qk, v, g, batch, seq):
    t = qk.shape[0]
    c = min(RET_CHUNK, seq)
    n = seq // c
    h, dk, dv = RET_HEADS, RET_QK_DIM, RET_V_DIM
    dmask, q_decay, k_decay, chunk_decay = _retention_tables(c)
    row = lambda b, hh, nn: b * n + nn
    return pl.pallas_call(
        _retention_kernel,
        grid=(batch, h, n),
        in_specs=[
            pl.BlockSpec((c, dk), lambda b, hh, nn: (row(b, hh, nn), hh)),
            pl.BlockSpec((c, dk), lambda b, hh, nn: (row(b, hh, nn), h + hh)),
            pl.BlockSpec((c, dv), lambda b, hh, nn: (row(b, hh, nn), hh)),
            pl.BlockSpec((c, dv), lambda b, hh, nn: (row(b, hh, nn), hh)),
            pl.BlockSpec((1, c, c), lambda b, hh, nn: (hh, 0, 0)),
            pl.BlockSpec((1, c, 1), lambda b, hh, nn: (hh, 0, 0)),
            pl.BlockSpec((1, c, 1), lambda b, hh, nn: (hh, 0, 0)),
            pl.BlockSpec((1, 1, 1), lambda b, hh, nn: (hh, 0, 0)),
        ],
        out_specs=pl.BlockSpec((c, dv), lambda b, hh, nn: (row(b, hh, nn), hh)),
        out_shape=jax.ShapeDtypeStruct((t, h * dv), BF16),
        scratch_shapes=[pltpu.VMEM((dk, dv), F32)],
        compiler_params=_cparams("parallel", "parallel", "arbitrary"),
        name="retention",
    )(qk, qk, v, g, dmask, q_decay, k_decay, chunk_decay)


def _diff_attn_kernel(q_ref, k_ref, v_ref, lam_ref, sg_ref, o_ref, qm_ref, m_ref, l_ref, acc_ref,
                      *, lambda_init):
    i = pl.program_id(2)
    j = pl.program_id(3)
    tq = q_ref.shape[0]
    tk = k_ref.shape[0]

    @pl.when(j == 0)
    def _():
        q = q_ref[...]
        lane = lax.broadcasted_iota(jnp.int32, q.shape, 1)
        zero = jnp.zeros_like(q)
        qm_ref[:tq, :] = jnp.where(lane < DIFF_HEAD_DIM, q, zero)
        qm_ref[tq:, :] = jnp.where(lane >= DIFF_HEAD_DIM, q, zero)
        m_ref[...] = jnp.full_like(m_ref, NEG_BIG)
        l_ref[...] = jnp.zeros_like(l_ref)
        acc_ref[...] = jnp.zeros_like(acc_ref)

    def step(masked):
        s = lax.dot_general(qm_ref[...], k_ref[...], (((1,), (1,)), ((), ())),
                            preferred_element_type=F32)
        if masked:
            qpos = lax.broadcasted_iota(jnp.int32, (2 * tq, tk), 0) % tq
            kpos = lax.broadcasted_iota(jnp.int32, (2 * tq, tk), 1)
            s = jnp.where(kpos <= qpos, s, NEG_BIG)
        m_old = m_ref[...]
        m_new = jnp.maximum(m_old, jnp.max(s, -1, keepdims=True))
        alpha = jnp.exp(m_old - m_new)
        p = jnp.exp(s - m_new)
        l_ref[...] = alpha * l_ref[...] + jnp.sum(p, -1, keepdims=True)
        acc_ref[...] = alpha * acc_ref[...] + jnp.dot(p.astype(BF16), v_ref[...],
                                                      preferred_element_type=F32)
        m_ref[...] = m_new

    @pl.when(j < i)
    def _():
        step(False)

    @pl.when(j == i)
    def _():
        step(True)
        lam = lam_ref[...]
        lam_full = (jnp.exp(jnp.sum(lam[0:1] * lam[1:2], -1, keepdims=True))
                    - jnp.exp(jnp.sum(lam[2:3] * lam[3:4], -1, keepdims=True)) + lambda_init)
        o1 = acc_ref[:tq, :] / l_ref[:tq, :]
        o2 = acc_ref[tq:, :] / l_ref[tq:, :]
        o = o1 - lam_full * o2
        o = o * lax.rsqrt(jnp.mean(o * o, -1, keepdims=True) + LN_EPS) * sg_ref[...] * (1.0 - lambda_init)
        o_ref[...] = o.astype(o_ref.dtype)


def _diff_attention(q, k, v, lam, subln_g, lambda_init, batch, seq):
    t, d = q.shape
    tq = min(ATTN_TILE, seq)
    nq = seq // tq
    hd = 2 * DIFF_HEAD_DIM
    return pl.pallas_call(
        functools.partial(_diff_attn_kernel, lambda_init=lambda_init),
        grid=(batch, DIFF_HEADS, nq, nq),
        in_specs=[
            pl.BlockSpec((tq, hd), lambda b, h, i, j: (b * nq + i, h)),
            pl.BlockSpec((tq, hd), lambda b, h, i, j: (b * nq + jnp.minimum(j, i), h)),
            pl.BlockSpec((tq, hd), lambda b, h, i, j: (b * nq + jnp.minimum(j, i), h)),
            pl.BlockSpec((4, DIFF_HEAD_DIM), lambda b, h, i, j: (0, 0)),
            pl.BlockSpec((1, hd), lambda b, h, i, j: (0, 0)),
        ],
        out_specs=pl.BlockSpec((tq, hd), lambda b, h, i, j: (b * nq + i, h)),
        out_shape=jax.ShapeDtypeStruct((t, d), BF16),
        scratch_shapes=[
            pltpu.VMEM((2 * tq, hd), BF16),
            pltpu.VMEM((2 * tq, 1), F32),
            pltpu.VMEM((2 * tq, 1), F32),
            pltpu.VMEM((2 * tq, hd), F32),
        ],
        compiler_params=_cparams("parallel", "parallel", "parallel", "arbitrary"),
        name="diff_attn",
    )(q, k, v, lam, subln_g.reshape(1, hd))


def _router_kernel(h_ref, w_ref, b_ref, idx_ref, gate_ref):
    logits = jnp.dot(h_ref[...], w_ref[...], preferred_element_type=F32,
                     precision=lax.Precision.HIGHEST) + b_ref[...]
    lane = lax.broadcasted_iota(jnp.int32, logits.shape, 1)
    cur = jnp.where(lane < N_EXPERTS, logits, -jnp.inf)
    vals, idxs = [], []
    for _ in range(TOP_K):
        m = jnp.max(cur, -1, keepdims=True)
        sel = jnp.min(jnp.where(cur == m, lane, ROUTER_PAD), -1, keepdims=True)
        vals.append(m)
        idxs.append(sel)
        cur = jnp.where(lane == sel, -jnp.inf, cur)
    es = [jnp.exp(vk - vals[0]) for vk in vals]
    denom = es[0] + es[1] + es[2] + es[3]
    idx_out = jnp.zeros(logits.shape, jnp.int32)
    gate_out = jnp.zeros(logits.shape, F32)
    for kk in range(TOP_K):
        idx_out = jnp.where(lane == kk, idxs[kk], idx_out)
        gate_out = jnp.where(lane == kk, es[kk] / denom, gate_out)
    idx_ref[...] = idx_out
    gate_ref[...] = gate_out


def _router(h, w_router, b_router):
    t, d = h.shape
    tm = min(ROW_TILE, t)
    w = jnp.zeros((d, ROUTER_PAD), F32).at[:, :N_EXPERTS].set(w_router)
    b = jnp.zeros((1, ROUTER_PAD), F32).at[0, :N_EXPERTS].set(b_router)
    idx, gate = pl.pallas_call(
        _router_kernel,
        grid=(t // tm,),
        in_specs=[
            pl.BlockSpec((tm, d), lambda i: (i, 0)),
            pl.BlockSpec((d, ROUTER_PAD), lambda i: (0, 0)),
            pl.BlockSpec((1, ROUTER_PAD), lambda i: (0, 0)),
        ],
        out_specs=[
            pl.BlockSpec((tm, ROUTER_PAD), lambda i: (i, 0)),
            pl.BlockSpec((tm, ROUTER_PAD), lambda i: (i, 0)),
        ],
        out_shape=[
            jax.ShapeDtypeStruct((t, ROUTER_PAD), jnp.int32),
            jax.ShapeDtypeStruct((t, ROUTER_PAD), F32),
        ],
        compiler_params=_cparams("parallel"),
        name="router",
    )(h, w, b)
    return idx[:, :TOP_K], gate[:, :TOP_K]


def _expert_kernel(be_ref, x_ref, wu_ref, bu_ref, wd_ref, bd_ref, rg_ref, o_ref):
    del be_ref
    hdn = jnp.dot(x_ref[...], wu_ref[0], preferred_element_type=F32) + bu_ref[0]
    glu = jnp.minimum(hdn[:, :D_FF], SWIGLU_LIMIT)
    lin = jnp.clip(hdn[:, D_FF:], -SWIGLU_LIMIT, SWIGLU_LIMIT)
    act = glu * jax.nn.sigmoid(SWIGLU_ALPHA * glu) * (lin + 1.0)
    y = jnp.dot(act.astype(BF16), wd_ref[0], preferred_element_type=F32) + bd_ref[0]
    o_ref[...] = y * rg_ref[...]


def _experts(xs, block_e, row_gate, w_up, b_up, w_down, b_down):
    n_rows, d = xs.shape
    nb = n_rows // MOE_ROWS
    e = w_up.shape[0]
    grid_spec = pltpu.PrefetchScalarGridSpec(
        num_scalar_prefetch=1,
        grid=(nb,),
        in_specs=[
            pl.BlockSpec((MOE_ROWS, d), lambda i, be: (i, 0)),
            pl.BlockSpec((1, d, 2 * D_FF), lambda i, be: (be[i], 0, 0)),
            pl.BlockSpec((1, 1, 2 * D_FF), lambda i, be: (be[i], 0, 0)),
            pl.BlockSpec((1, D_FF, d), lambda i, be: (be[i], 0, 0)),
            pl.BlockSpec((1, 1, d), lambda i, be: (be[i], 0, 0)),
            pl.BlockSpec((MOE_ROWS, 1), lambda i, be: (i, 0)),
        ],
        out_specs=pl.BlockSpec((MOE_ROWS, d), lambda i, be: (i, 0)),
    )
    return pl.pallas_call(
        _expert_kernel,
        grid_spec=grid_spec,
        out_shape=jax.ShapeDtypeStruct((n_rows, d), F32),
        compiler_params=_cparams("arbitrary"),
        name="experts",
    )(block_e, xs, w_up, b_up.reshape(e, 1, 2 * D_FF), w_down, b_down.reshape(e, 1, d), row_gate)


def _moe(h, w_router, b_router, w_up, b_up, w_down, b_down):
    t, d = h.shape
    blk = MOE_ROWS
    top_idx, gates = _router(h, w_router, b_router)
    flat_e = top_idx.reshape(-1)
    flat_tok = jnp.arange(t * TOP_K, dtype=jnp.int32) // TOP_K
    flat_gate = gates.reshape(-1)
    order = jnp.argsort(flat_e)
    sorted_e = flat_e[order]
    counts = jnp.zeros((N_EXPERTS,), jnp.int32).at[flat_e].add(1)
    offsets = jnp.cumsum(counts) - counts
    padded = (counts + blk - 1) // blk * blk
    padded_end = jnp.cumsum(padded)
    padded_off = padded_end - padded
    rank = jnp.arange(t * TOP_K, dtype=jnp.int32) - offsets[sorted_e]
    dest = padded_off[sorted_e] + rank
    n_rows = t * TOP_K + N_EXPERTS * blk
    n_blocks = n_rows // blk
    row_tok = jnp.full((n_rows,), t, jnp.int32).at[dest].set(flat_tok[order])
    row_gate = jnp.zeros((n_rows,), F32).at[dest].set(flat_gate[order])
    block_e = jnp.minimum(
        jnp.searchsorted(padded_end, jnp.arange(n_blocks, dtype=jnp.int32) * blk, side="right"),
        N_EXPERTS - 1).astype(jnp.int32)
    x_pad = jnp.concatenate([h.astype(BF16), jnp.zeros((1, d), BF16)], 0)
    xs = x_pad[row_tok]
    ys = _experts(xs, block_e, row_gate.reshape(n_rows, 1), w_up, b_up, w_down, b_down)
    return jax.ops.segment_sum(ys, row_tok, num_segments=t + 1)[:t]


def _ret_rope_tables(seq):
    inv_freq = 1.0 / (RET_ROPE_THETA ** jnp.linspace(0.0, 1.0, RET_QK_DIM // 2, dtype=F32))
    ang = jnp.arange(seq, dtype=F32)[:, None] * inv_freq[None, :]
    cos, sin = jnp.cos(ang), jnp.sin(ang)
    return jnp.concatenate([cos, cos], -1), jnp.concatenate([-sin, sin], -1)


def _partial_rope_tables(seq):
    half = ROPE_DIM // 2
    inv_freq = 1.0 / (ROPE_THETA ** (jnp.arange(0, ROPE_DIM, 2, dtype=F32) / ROPE_DIM))
    ang = jnp.arange(seq, dtype=F32)[:, None] * inv_freq[None, :]
    cos, sin = jnp.cos(ang), jnp.sin(ang)
    pad = jnp.zeros((seq, DIFF_HEAD_DIM - ROPE_DIM), F32)
    c64 = jnp.concatenate([cos, cos, pad + 1.0], -1)
    s64 = jnp.concatenate([-sin, sin, pad], -1)
    del half
    return jnp.concatenate([c64, c64], -1), jnp.concatenate([s64, s64], -1)


def kernel(x, ret_w_in, ret_w_out, kv_w, diff_w_q, diff_lambda, diff_subln_g, diff_w_out,
           ln_attn_g, ln_attn_b, ln_ffn_g, ln_ffn_b, moe_w_router, moe_b_router,
           moe_w_up, moe_b_up, moe_w_down, moe_b_down):
    batch, seq, d = x.shape
    t = batch * seq
    h = x.reshape(t, d)
    ret_cos, ret_sin = _ret_rope_tables(seq)
    par_cos, par_sin = _partial_rope_tables(seq)
    qk_scale = jnp.concatenate([jnp.ones((1, d), F32), jnp.full((1, d), RET_QK_DIM ** -0.5, F32)], -1)
    k_sh = v_sh = None
    for l in range(DEPTH):
        if l < N_A_LAYERS:
            w_in = ret_w_in[l].astype(BF16)
            qk = _proj(h, w_in[:, :2 * d], BF16, rope="half", cos=ret_cos, sin=ret_sin,
                       scale=qk_scale, seq=seq)
            v = _proj(h, w_in[:, 2 * d:2 * d + RET_V_WIDTH], BF16)
            g = _proj(h, w_in[:, 2 * d + RET_V_WIDTH:], F32)
            a = _retention(qk, v, g, batch, seq)
            h = _outproj_ln(a, ret_w_out[l].astype(BF16), h, ln_attn_g[l], ln_attn_b[l])
        else:
            j = l - N_A_LAYERS
            lambda_init = 0.8 - 0.6 * math.exp(-0.3 * l)
            q = _proj(h, diff_w_q[j].astype(BF16), BF16, rope="partial", cos=par_cos, sin=par_sin,
                      scale=jnp.full((1, d), DIFF_HEAD_DIM ** -0.5, F32), seq=seq)
            a = _diff_attention(q, k_sh, v_sh, diff_lambda[j], diff_subln_g[j], lambda_init, batch, seq)
            h = _outproj_ln(a, diff_w_out[j].astype(BF16), h, ln_attn_g[l], ln_attn_b[l])
        ffn = _moe(h, moe_w_router[l], moe_b_router[l], moe_w_up[l].astype(BF16), moe_b_up[l],
                   moe_w_down[l].astype(BF16), moe_b_down[l])
        h = _residual_ln(h, ffn, ln_ffn_g[l], ln_ffn_b[l])
        if l == N_A_LAYERS - 1:
            kvw = kv_w.astype(BF16)
            k_sh = _proj(h, kvw[:, :d], BF16, rope="partial", cos=par_cos, sin=par_sin, seq=seq)
            v_sh = _proj(h, kvw[:, d:], BF16)
    return h.reshape(batch, seq, d)
```

```python
import functools
import math

import jax
import jax.numpy as jnp
from jax import lax
from jax.experimental import pallas as pl
from jax.experimental.pallas import tpu as pltpu

F32 = jnp.float32
BF16 = jnp.bfloat16

D_MODEL = 1024
DEPTH = 4
N_A_LAYERS = DEPTH // 2
RET_HEADS = 8
RET_QK_DIM = D_MODEL // RET_HEADS
RET_V_DIM = 2 * RET_QK_DIM
RET_V_WIDTH = RET_HEADS * RET_V_DIM
RET_ROPE_THETA = 10000.0
DIFF_HEAD_DIM = 64
DIFF_HEADS = D_MODEL // (2 * DIFF_HEAD_DIM)
DIFF_V_DIM = 2 * DIFF_HEAD_DIM
ROPE_THETA = 500000.0
ROPE_DIM = DIFF_HEAD_DIM // 4
NEG_BIG = -1e30
N_EXPERTS = 32
TOP_K = 4
D_FF = D_MODEL
SWIGLU_ALPHA = 1.702
SWIGLU_LIMIT = 7.0
DN_ALPHA = (2 * DEPTH) ** 0.25
LN_EPS = 1e-5

LANES = 128
SUBLANES = 8
VMEM_LIMIT = 48 * 1024 * 1024
EXPERT_VMEM_LIMIT = 58 * 1024 * 1024

ROW_TILE = 512
RET_CHUNK = 256
ATTN_TILE = 512
ATTN_GROUP = 256
MOE_ROWS = 512
MOE_TILE = 512
ROW_ALIGN = SUBLANES
ROUTER_PAD = LANES


def _cparams(*sem):
    return pltpu.CompilerParams(dimension_semantics=sem, vmem_limit_bytes=VMEM_LIMIT)


def _round_up(x, m):
    return (x + m - 1) // m * m


def _layer_norm(y, g, b):
    mu = jnp.mean(y, -1, keepdims=True)
    d = y - mu
    var = jnp.mean(d * d, -1, keepdims=True)
    return d * lax.rsqrt(var + LN_EPS) * g + b


def _proj_kernel(x_ref, w_ref, c_ref, s_ref, scale_ref, o_ref, *, rope):
    acc = jnp.dot(x_ref[...].astype(BF16), w_ref[...], preferred_element_type=F32)
    tn = acc.shape[1]
    if rope is None:
        o_ref[...] = acc.astype(o_ref.dtype)
        return
    c = c_ref[...]
    s = s_ref[...]
    lane = lax.broadcasted_iota(jnp.int32, (1, LANES), 1)
    for g in range(tn // LANES):
        sl = slice(g * LANES, (g + 1) * LANES)
        xg = acc[:, sl]
        if rope == "half":
            y = xg * c + pltpu.roll(xg, LANES // 2, 1) * s
        else:
            half = ROPE_DIM // 2
            up = pltpu.roll(xg, LANES - half, 1)
            dn = pltpu.roll(xg, half, 1)
            partner = jnp.where((lane & half) == 0, up, dn)
            y = jnp.where((lane & (DIFF_HEAD_DIM - 1)) < ROPE_DIM, xg * c + partner * s, xg)
        o_ref[:, sl] = (y * scale_ref[:, sl]).astype(o_ref.dtype)


def _proj(x, w, out_dtype, *, rope=None, cos=None, sin=None, scale=None, seq=None, tn=1024):
    t, k = x.shape
    n = w.shape[1]
    tm = min(ROW_TILE, t)
    tn = min(tn, n)
    if cos is None:
        cos = jnp.zeros((tm, LANES), F32)
        sin = cos
        pos_blocks = 1
    else:
        pos_blocks = seq // tm
    if scale is None:
        scale = jnp.ones((1, n), F32)
    return pl.pallas_call(
        functools.partial(_proj_kernel, rope=rope),
        grid=(n // tn, t // tm),
        in_specs=[
            pl.BlockSpec((tm, k), lambda j, i: (i, 0)),
            pl.BlockSpec((k, tn), lambda j, i: (0, j)),
            pl.BlockSpec((tm, LANES), lambda j, i: (i % pos_blocks, 0)),
            pl.BlockSpec((tm, LANES), lambda j, i: (i % pos_blocks, 0)),
            pl.BlockSpec((1, tn), lambda j, i: (0, j)),
        ],
        out_specs=pl.BlockSpec((tm, tn), lambda j, i: (i, j)),
        out_shape=jax.ShapeDtypeStruct((t, n), out_dtype),
        compiler_params=_cparams("parallel", "parallel"),
        name="proj",
    )(x, w, cos, sin, scale)


def _outproj_ln_kernel(a_ref, w_ref, h_ref, g_ref, b_ref, o_ref):
    mix = jnp.dot(a_ref[...], w_ref[...], preferred_element_type=F32)
    o_ref[...] = _layer_norm(DN_ALPHA * h_ref[...] + mix, g_ref[...], b_ref[...])


def _outproj_ln(a, w, h, g, b):
    t, k = a.shape
    d = w.shape[1]
    tm = min(ROW_TILE, t)
    return pl.pallas_call(
        _outproj_ln_kernel,
        grid=(t // tm,),
        in_specs=[
            pl.BlockSpec((tm, k), lambda i: (i, 0)),
            pl.BlockSpec((k, d), lambda i: (0, 0)),
            pl.BlockSpec((tm, d), lambda i: (i, 0)),
            pl.BlockSpec((1, d), lambda i: (0, 0)),
            pl.BlockSpec((1, d), lambda i: (0, 0)),
        ],
        out_specs=pl.BlockSpec((tm, d), lambda i: (i, 0)),
        out_shape=jax.ShapeDtypeStruct((t, d), F32),
        compiler_params=_cparams("parallel"),
        name="outproj_ln",
    )(a, w, h, g.reshape(1, d), b.reshape(1, d))


def _retention_kernel(q_ref, k_ref, v_ref, g_ref, dm_ref, qd_ref, kd_ref, cd_ref, o_ref, state_ref):
    @pl.when(pl.program_id(2) == 0)
    def _():
        state_ref[...] = jnp.zeros_like(state_ref)

    q = q_ref[...]
    k = k_ref[...]
    v = v_ref[...]
    s = lax.dot_general(q, k, (((1,), (1,)), ((), ())), preferred_element_type=F32)
    s = s * dm_ref[0]
    o = jnp.dot(s.astype(BF16), v, preferred_element_type=F32)
    state = state_ref[...]
    cross = jnp.dot(q, state.astype(BF16), preferred_element_type=F32)
    o = o + cross * qd_ref[0]
    kdec = (k.astype(F32) * kd_ref[0]).astype(BF16)
    kv = lax.dot_general(kdec, v, (((0,), (0,)), ((), ())), preferred_element_type=F32)
    state_ref[...] = cd_ref[0] * state + kv
    mu = jnp.mean(o, -1, keepdims=True)
    d = o - mu
    var = jnp.mean(d * d, -1, keepdims=True)
    gate = g_ref[...]
    o_ref[...] = (d * lax.rsqrt(var + LN_EPS) * (gate * jax.nn.sigmoid(gate))).astype(o_ref.dtype)


def _retention_tables(c):
    h = RET_HEADS
    log_gamma = jnp.log1p(-jnp.exp2(-5.0 - jnp.arange(h, dtype=F32)))
    idx = jnp.arange(c, dtype=F32)
    rel = idx[:, None] - idx[None, :]
    dmask = jnp.where(rel >= 0, jnp.exp(jnp.maximum(rel, 0.0) * log_gamma[:, None, None]), 0.0)
    k_decay = jnp.exp((c - 1 - idx) * log_gamma[:, None])[:, :, None]
    q_decay = jnp.exp((idx + 1) * log_gamma[:, None])[:, :, None]
    chunk_decay = jnp.exp(c * log_gamma)[:, None, None]
    return dmask, q_decay, k_decay, chunk_decay


def _retention(qk, v, g, batch, seq):
    t = qk.shape[0]
    c = min(RET_CHUNK, seq)
    n = seq // c
    h, dk, dv = RET_HEADS, RET_QK_DIM, RET_V_DIM
    dmask, q_decay, k_decay, chunk_decay = _retention_tables(c)
    row = lambda b, hh, nn: b * n + nn
    return pl.pallas_call(
        _retention_kernel,
        grid=(batch, h, n),
        in_specs=[
            pl.BlockSpec((c, dk), lambda b, hh, nn: (row(b, hh, nn), hh)),
            pl.BlockSpec((c, dk), lambda b, hh, nn: (row(b, hh, nn), h + hh)),
            pl.BlockSpec((c, dv), lambda b, hh, nn: (row(b, hh, nn), hh)),
            pl.BlockSpec((c, dv), lambda b, hh, nn: (row(b, hh, nn), hh)),
            pl.BlockSpec((1, c, c), lambda b, hh, nn: (hh, 0, 0)),
            pl.BlockSpec((1, c, 1), lambda b, hh, nn: (hh, 0, 0)),
            pl.BlockSpec((1, c, 1), lambda b, hh, nn: (hh, 0, 0)),
            pl.BlockSpec((1, 1, 1), lambda b, hh, nn: (hh, 0, 0)),
        ],
        out_specs=pl.BlockSpec((c, dv), lambda b, hh, nn: (row(b, hh, nn), hh)),
        out_shape=jax.ShapeDtypeStruct((t, h * dv), BF16),
        scratch_shapes=[pltpu.VMEM((dk, dv), F32)],
        compiler_params=_cparams("parallel", "parallel", "arbitrary"),
        name="retention",
    )(qk, qk, v, g, dmask, q_decay, k_decay, chunk_decay)


def _diff_attn_kernel(q_ref, k_ref, vt_ref, lam_ref, sg_ref, o_ref, qm_ref, m_ref, l_ref, acc_ref,
                      *, lambda_init):
    i = pl.program_id(2)
    tq = q_ref.shape[0]
    tk = vt_ref.shape[2]
    gw = min(ATTN_GROUP, tq)

    q = q_ref[...]
    lane = lax.broadcasted_iota(jnp.int32, q.shape, 1)
    zero = jnp.zeros_like(q)
    qm_ref[:tq, :] = jnp.where(lane < DIFF_HEAD_DIM, q, zero)
    qm_ref[tq:, :] = jnp.where(lane >= DIFF_HEAD_DIM, q, zero)
    m_ref[...] = jnp.full_like(m_ref, NEG_BIG)
    l_ref[...] = jnp.zeros_like(l_ref)
    acc_ref[...] = jnp.zeros_like(acc_ref)

    def step(j, masked):
        off = pl.multiple_of(j * tk, tk)
        kt = k_ref[pl.ds(off, tk), :]
        vt = vt_ref[j]
        for g in range(2 * tq // gw):
            cols = slice(g * gw, (g + 1) * gw)
            s = lax.dot_general(kt, qm_ref[cols, :], (((1,), (1,)), ((), ())),
                                preferred_element_type=F32)
            if masked:
                kpos = lax.broadcasted_iota(jnp.int32, s.shape, 0)
                qpos = (lax.broadcasted_iota(jnp.int32, s.shape, 1) + g * gw) & (tq - 1)
                s = jnp.where(kpos <= qpos, s, NEG_BIG)
            m_old = m_ref[:, cols]
            m_new = jnp.maximum(m_old, jnp.max(s, 0, keepdims=True))
            alpha = jnp.exp2(m_old - m_new)
            p = jnp.exp2(s - m_new)
            l_ref[:, cols] = alpha * l_ref[:, cols] + jnp.sum(p, 0, keepdims=True)
            acc_ref[:, cols] = alpha * acc_ref[:, cols] + jnp.dot(vt, p.astype(BF16),
                                                                  preferred_element_type=F32)
            m_ref[:, cols] = m_new

    def body(j, carry):
        step(j, False)
        return carry

    lax.fori_loop(0, i, body, 0)
    step(i, True)

    lam = lam_ref[...]
    lam_full = (jnp.exp(jnp.sum(lam[0:1] * lam[1:2], -1, keepdims=True))
                - jnp.exp(jnp.sum(lam[2:3] * lam[3:4], -1, keepdims=True)) + lambda_init)
    o1 = acc_ref[:, :tq] / l_ref[:, :tq]
    o2 = acc_ref[:, tq:] / l_ref[:, tq:]
    o = o1 - lam_full * o2
    o = o * lax.rsqrt(jnp.mean(o * o, 0, keepdims=True) + LN_EPS) * sg_ref[...] * (1.0 - lambda_init)
    o_ref[...] = o.T.astype(o_ref.dtype)


def _diff_attention(q, k, vt, lam, subln_g, lambda_init, batch, seq):
    t, d = q.shape
    tq = min(ATTN_TILE, seq)
    nq = seq // tq
    hd = 2 * DIFF_HEAD_DIM
    return pl.pallas_call(
        functools.partial(_diff_attn_kernel, lambda_init=lambda_init),
        grid=(batch, DIFF_HEADS, nq),
        in_specs=[
            pl.BlockSpec((tq, hd), lambda b, h, i: (b * nq + i, h)),
            pl.BlockSpec((seq, hd), lambda b, h, i: (b, h)),
            pl.BlockSpec((nq, hd, tq), lambda b, h, i: (b * DIFF_HEADS + h, 0, 0)),
            pl.BlockSpec((4, DIFF_HEAD_DIM), lambda b, h, i: (0, 0)),
            pl.BlockSpec((hd, 1), lambda b, h, i: (0, 0)),
        ],
        out_specs=pl.BlockSpec((tq, hd), lambda b, h, i: (b * nq + i, h)),
        out_shape=jax.ShapeDtypeStruct((t, d), BF16),
        scratch_shapes=[
            pltpu.VMEM((2 * tq, hd), BF16),
            pltpu.VMEM((1, 2 * tq), F32),
            pltpu.VMEM((1, 2 * tq), F32),
            pltpu.VMEM((hd, 2 * tq), F32),
        ],
        compiler_params=_cparams("parallel", "parallel", "arbitrary"),
        name="diff_attn",
    )(q, k, vt, lam, subln_g.reshape(hd, 1))


def _transpose_values(v, batch, seq):
    tk = min(ATTN_TILE, seq)
    nk = seq // tk
    hd = 2 * DIFF_HEAD_DIM
    v5 = v.reshape(batch, nk, tk, DIFF_HEADS, hd).transpose(0, 3, 1, 4, 2)
    return v5.reshape(batch * DIFF_HEADS * nk, hd, tk)


def _router_kernel(h_ref, w_ref, b_ref, idx_ref, gate_ref, cnt_ref):
    logits = jnp.dot(h_ref[...], w_ref[...], preferred_element_type=F32,
                     precision=lax.Precision.HIGHEST) + b_ref[...]
    lt = logits.T
    tm = lt.shape[1]
    row = lax.broadcasted_iota(jnp.int32, lt.shape, 0)
    cur = jnp.where(row < N_EXPERTS, lt, -jnp.inf)
    vals, idxs = [], []
    for _ in range(TOP_K):
        m = jnp.max(cur, 0, keepdims=True)
        sel = jnp.min(jnp.where(cur == m, row, ROUTER_PAD), 0, keepdims=True)
        vals.append(m)
        idxs.append(sel)
        cur = jnp.where(row == sel, -jnp.inf, cur)
    es = [jnp.exp(vk - vals[0]) for vk in vals]
    denom = es[0] + es[1] + es[2] + es[3]
    row8 = lax.broadcasted_iota(jnp.int32, (SUBLANES, tm), 0)
    idx_out = jnp.full((SUBLANES, tm), -1, jnp.int32)
    gate_out = jnp.zeros((SUBLANES, tm), F32)
    chosen = jnp.zeros(lt.shape, F32)
    for kk in range(TOP_K):
        idx_out = jnp.where(row8 == kk, idxs[kk], idx_out)
        gate_out = jnp.where(row8 == kk, es[kk] / denom, gate_out)
        chosen = jnp.where(row == idxs[kk], 1.0, chosen)
    idx_ref[...] = idx_out
    gate_ref[...] = gate_out
    ones = jnp.ones((SUBLANES, tm), BF16)
    cnt = lax.dot_general(ones, chosen.astype(BF16), (((1,), (1,)), ((), ())),
                          preferred_element_type=F32)
    cnt_ref[0] = cnt.astype(jnp.int32)


def _router(h, w_router, b_router):
    t, d = h.shape
    tm = min(MOE_TILE, t)
    nt = t // tm
    w = jnp.zeros((d, ROUTER_PAD), F32).at[:, :N_EXPERTS].set(w_router)
    b = jnp.zeros((1, ROUTER_PAD), F32).at[0, :N_EXPERTS].set(b_router)
    idx, gate, cnt = pl.pallas_call(
        _router_kernel,
        grid=(nt,),
        in_specs=[
            pl.BlockSpec((tm, d), lambda i: (i, 0)),
            pl.BlockSpec((d, ROUTER_PAD), lambda i: (0, 0)),
            pl.BlockSpec((1, ROUTER_PAD), lambda i: (0, 0)),
        ],
        out_specs=[
            pl.BlockSpec((SUBLANES, tm), lambda i: (0, i)),
            pl.BlockSpec((SUBLANES, tm), lambda i: (0, i)),
            pl.BlockSpec((1, SUBLANES, ROUTER_PAD), lambda i: (i, 0, 0)),
        ],
        out_shape=[
            jax.ShapeDtypeStruct((SUBLANES, t), jnp.int32),
            jax.ShapeDtypeStruct((SUBLANES, t), F32),
            jax.ShapeDtypeStruct((nt, SUBLANES, ROUTER_PAD), jnp.int32),
        ],
        compiler_params=_cparams("parallel"),
        name="router",
    )(h, w, b)
    return idx, gate, cnt[:, 0, :N_EXPERTS]


def _moe_plan(cnt, n_rows):
    a, blk = ROW_ALIGN, MOE_ROWS
    cnt_a = _round_up(cnt, a)
    rows_e = jnp.sum(cnt_a, 0)
    cap_e = _round_up(rows_e, blk)
    seg_end = jnp.cumsum(cap_e)
    seg_start = seg_end - cap_e
    gbase = seg_start[None, :] + jnp.cumsum(cnt_a, 0) - cnt_a
    loff = jnp.cumsum(cnt_a, 1) - cnt_a
    nch = cnt_a // a
    nb = n_rows // blk
    n_used = seg_end[-1] // blk
    block_row = jnp.minimum(jnp.arange(nb, dtype=jnp.int32), n_used - 1) * blk
    block_e = jnp.minimum(jnp.searchsorted(seg_end, block_row, side="right"), N_EXPERTS - 1)
    loffv = jnp.zeros((cnt.shape[0], ROUTER_PAD, 1), F32).at[:, :N_EXPERTS, 0].set(loff.astype(F32))
    i32 = lambda x: x.astype(jnp.int32)
    return dict(nch=i32(nch), gbase=i32(gbase), loff=i32(loff), ntot=i32(jnp.sum(nch, 1)),
                pad_start=i32(seg_start + rows_e), pad_n=i32((cap_e - rows_e) // a),
                block_e=i32(block_e), n_used=i32(n_used).reshape(1), loffv=loffv)


def _local_positions(idx, loffv, u):
    tm = idx.shape[1]
    row = lax.broadcasted_iota(jnp.int32, (ROUTER_PAD, tm), 0)
    hits = [row == idx[kk:kk + 1, :] for kk in range(TOP_K)]
    chosen = jnp.zeros((ROUTER_PAD, tm), F32)
    for hit in hits:
        chosen = jnp.where(hit, 1.0, chosen)
    before = jnp.dot(chosen.astype(BF16), u, preferred_element_type=F32)
    pos = before + loffv
    return [jnp.sum(jnp.where(hit, pos, 0.0), 0, keepdims=True).astype(jnp.int32) for hit in hits]


def _selection_matrix(js, vals, n_local, tm):
    rr = lax.broadcasted_iota(jnp.int32, (n_local, tm), 0)
    out = jnp.zeros((n_local, tm), F32)
    for j, v in zip(js, vals):
        out = jnp.where(rr == j, v, out)
    return out


def _run_copies(nch_s, src_s, dst_s, i, copy):
    def per_expert(e, carry):
        src0 = src_s[i, e]
        dst0 = dst_s[i, e]

        def per_chunk(c, carry2):
            copy(pl.multiple_of(src0 + c * ROW_ALIGN, ROW_ALIGN),
                 pl.multiple_of(dst0 + c * ROW_ALIGN, ROW_ALIGN)).start()
            return carry2

        return lax.fori_loop(0, nch_s[i, e], per_chunk, carry)

    lax.fori_loop(0, N_EXPERTS, per_expert, 0)


def _drain(n, copy):
    def body(c, carry):
        copy(0, 0).wait()
        return carry

    lax.fori_loop(0, n, body, 0)


def _dispatch_kernel(nch_s, gbase_s, loff_s, ntot_s, pstart_s, pn_s, nu_s,
                     h_ref, idx_ref, loffv_ref, u_ref, xs_hbm, loc_ref, zero_ref, sem, zsem):
    i = pl.program_id(0)
    n_local, tm = loc_ref.shape[0], h_ref.shape[0]
    js = _local_positions(idx_ref[...], loffv_ref[0], u_ref[...])
    perm = _selection_matrix(js, [1.0] * TOP_K, n_local, tm).astype(BF16)
    loc_ref[...] = jnp.dot(perm, h_ref[...].astype(BF16), preferred_element_type=F32)

    def copy(src_row, dst_row):
        return pltpu.make_async_copy(loc_ref.at[pl.ds(src_row, ROW_ALIGN)],
                                     xs_hbm.at[pl.ds(dst_row, ROW_ALIGN)], sem)

    def zero_copy(src_row, dst_row):
        return pltpu.make_async_copy(zero_ref.at[pl.ds(src_row, ROW_ALIGN)],
                                     xs_hbm.at[pl.ds(dst_row, ROW_ALIGN)], zsem)

    def zero_block_copy(block):
        return pltpu.make_async_copy(
            zero_ref, xs_hbm.at[pl.ds(pl.multiple_of(block * MOE_ROWS, MOE_ROWS), MOE_ROWS)], zsem)

    _run_copies(nch_s, loff_s, gbase_s, i, copy)

    @pl.when(i == pl.num_programs(0) - 1)
    def _():
        zero_ref[...] = jnp.zeros_like(zero_ref)

        def per_expert(e, total):
            def per_chunk(c, carry):
                zero_copy(0, pl.multiple_of(pstart_s[e] + c * ROW_ALIGN, ROW_ALIGN)).start()
                return carry

            lax.fori_loop(0, pn_s[e], per_chunk, 0)
            return total + pn_s[e]

        total = lax.fori_loop(0, N_EXPERTS, per_expert, 0)
        _drain(total, zero_copy)
        n_blocks = xs_hbm.shape[0] // MOE_ROWS

        def start_block(blk, carry):
            zero_block_copy(blk).start()
            return carry

        def wait_block(blk, carry):
            zero_block_copy(blk).wait()
            return carry

        lax.fori_loop(nu_s[0], n_blocks, start_block, 0)
        lax.fori_loop(nu_s[0], n_blocks, wait_block, 0)

    _drain(ntot_s[i], copy)


def _dispatch(h, idx, plan, u, n_rows):
    t, d = h.shape
    tm = min(MOE_TILE, t)
    n_local = _round_up(TOP_K * tm + N_EXPERTS * (ROW_ALIGN - 1), 2 * LANES)
    grid_spec = pltpu.PrefetchScalarGridSpec(
        num_scalar_prefetch=7,
        grid=(t // tm,),
        in_specs=[
            pl.BlockSpec((tm, d), lambda i, *_: (i, 0)),
            pl.BlockSpec((SUBLANES, tm), lambda i, *_: (0, i)),
            pl.BlockSpec((1, ROUTER_PAD, 1), lambda i, *_: (i, 0, 0)),
            pl.BlockSpec((tm, tm), lambda i, *_: (0, 0)),
        ],
        out_specs=pl.BlockSpec(memory_space=pl.ANY),
        scratch_shapes=[
            pltpu.VMEM((n_local, d), F32),
            pltpu.VMEM((MOE_ROWS, d), F32),
            pltpu.SemaphoreType.DMA,
            pltpu.SemaphoreType.DMA,
        ],
    )
    return pl.pallas_call(
        _dispatch_kernel,
        grid_spec=grid_spec,
        out_shape=jax.ShapeDtypeStruct((n_rows, d), F32),
        compiler_params=_cparams("arbitrary"),
        name="dispatch",
    )(plan["nch"], plan["gbase"], plan["loff"], plan["ntot"], plan["pad_start"], plan["pad_n"],
      plan["n_used"], h, idx, plan["loffv"], u)


def _expert_kernel(be_ref, nu_ref, x_ref, wu_ref, bu_ref, wd_ref, bd_ref, o_ref, wu_bf, wd_bf):
    i = pl.program_id(0)

    @pl.when(jnp.logical_or(i == 0, be_ref[i] != be_ref[jnp.maximum(i - 1, 0)]))
    def _():
        wu_bf[...] = wu_ref[0, 0].astype(BF16)
        wd_bf[...] = wd_ref[0, 0].astype(BF16)

    @pl.when(i < nu_ref[0])
    def _():
        hdn = jnp.dot(x_ref[...].astype(BF16), wu_bf[...], preferred_element_type=F32) + bu_ref[0, 0]
        glu = jnp.minimum(hdn[:, :D_FF], SWIGLU_LIMIT)
        lin = jnp.clip(hdn[:, D_FF:], -SWIGLU_LIMIT, SWIGLU_LIMIT)
        act = glu * jax.nn.sigmoid(SWIGLU_ALPHA * glu) * (lin + 1.0)
        o_ref[...] = jnp.dot(act.astype(BF16), wd_bf[...], preferred_element_type=F32) + bd_ref[0, 0]

    @pl.when(i >= nu_ref[0])
    def _():
        o_ref[...] = jnp.zeros_like(o_ref)


def _experts(xs, plan, w_up, b_up, w_down, b_down, layer):
    n_rows, d = xs.shape
    depth, e = w_up.shape[:2]
    used = lambda i, nu: jnp.minimum(i, nu[0] - 1)
    grid_spec = pltpu.PrefetchScalarGridSpec(
        num_scalar_prefetch=2,
        grid=(n_rows // MOE_ROWS,),
        in_specs=[
            pl.BlockSpec((MOE_ROWS, d), lambda i, be, nu: (used(i, nu), 0)),
            pl.BlockSpec((1, 1, d, 2 * D_FF), lambda i, be, nu: (layer, be[i], 0, 0)),
            pl.BlockSpec((1, 1, 1, 2 * D_FF), lambda i, be, nu: (layer, be[i], 0, 0)),
            pl.BlockSpec((1, 1, D_FF, d), lambda i, be, nu: (layer, be[i], 0, 0)),
            pl.BlockSpec((1, 1, 1, d), lambda i, be, nu: (layer, be[i], 0, 0)),
        ],
        out_specs=pl.BlockSpec((MOE_ROWS, d), lambda i, be, nu: (i, 0)),
        scratch_shapes=[pltpu.VMEM((d, 2 * D_FF), BF16), pltpu.VMEM((D_FF, d), BF16)],
    )
    return pl.pallas_call(
        _expert_kernel,
        grid_spec=grid_spec,
        out_shape=jax.ShapeDtypeStruct((n_rows, d), F32),
        compiler_params=pltpu.CompilerParams(dimension_semantics=("arbitrary",),
                                             vmem_limit_bytes=EXPERT_VMEM_LIMIT),
        name="experts",
    )(plan["block_e"], plan["n_used"], xs, w_up, b_up.reshape(depth, e, 1, 2 * D_FF), w_down,
      b_down.reshape(depth, e, 1, d))


def _combine_kernel(nch_s, gbase_s, loff_s, ntot_s,
                    h_ref, idx_ref, gate_ref, loffv_ref, u_ref, g_ref, b_ref, ys_hbm,
                    o_ref, loc_ref, sem):
    i = pl.program_id(0)
    n_local, tm = loc_ref.shape[0], h_ref.shape[0]

    @pl.when(i == 0)
    def _():
        loc_ref[...] = jnp.zeros_like(loc_ref)

    def copy(dst_row, src_row):
        return pltpu.make_async_copy(ys_hbm.at[pl.ds(src_row, ROW_ALIGN)],
                                     loc_ref.at[pl.ds(dst_row, ROW_ALIGN)], sem)

    _run_copies(nch_s, loff_s, gbase_s, i, copy)
    js = _local_positions(idx_ref[...], loffv_ref[0], u_ref[...])
    gates = gate_ref[...]
    gmat = _selection_matrix(js, [gates[kk:kk + 1, :] for kk in range(TOP_K)], n_local, tm)
    _drain(ntot_s[i], copy)
    ffn = lax.dot_general(gmat.astype(BF16), loc_ref[...].astype(BF16), (((0,), (0,)), ((), ())),
                          preferred_element_type=F32)
    o_ref[...] = _layer_norm(DN_ALPHA * h_ref[...] + ffn, g_ref[...], b_ref[...])


def _combine_ln(h, ys, idx, gate, plan, u, g, b):
    t, d = h.shape
    tm = min(MOE_TILE, t)
    n_local = _round_up(TOP_K * tm + N_EXPERTS * (ROW_ALIGN - 1), 2 * LANES)
    grid_spec = pltpu.PrefetchScalarGridSpec(
        num_scalar_prefetch=4,
        grid=(t // tm,),
        in_specs=[
            pl.BlockSpec((tm, d), lambda i, *_: (i, 0)),
            pl.BlockSpec((SUBLANES, tm), lambda i, *_: (0, i)),
            pl.BlockSpec((SUBLANES, tm), lambda i, *_: (0, i)),
            pl.BlockSpec((1, ROUTER_PAD, 1), lambda i, *_: (i, 0, 0)),
            pl.BlockSpec((tm, tm), lambda i, *_: (0, 0)),
            pl.BlockSpec((1, d), lambda i, *_: (0, 0)),
            pl.BlockSpec((1, d), lambda i, *_: (0, 0)),
            pl.BlockSpec(memory_space=pl.ANY),
        ],
        out_specs=pl.BlockSpec((tm, d), lambda i, *_: (i, 0)),
        scratch_shapes=[pltpu.VMEM((n_local, d), F32), pltpu.SemaphoreType.DMA],
    )
    return pl.pallas_call(
        _combine_kernel,
        grid_spec=grid_spec,
        out_shape=jax.ShapeDtypeStruct((t, d), F32),
        compiler_params=_cparams("arbitrary"),
        name="combine_ln",
    )(plan["nch"], plan["gbase"], plan["loff"], plan["ntot"],
      h, idx, gate, plan["loffv"], u, g.reshape(1, d), b.reshape(1, d), ys)


def _moe_ln(h, w_router, b_router, w_up, b_up, w_down, b_down, layer, ln_g, ln_b):
    t, d = h.shape
    tm = min(MOE_TILE, t)
    nt = t // tm
    n_rows = _round_up(t * TOP_K + nt * N_EXPERTS * (ROW_ALIGN - 1) + N_EXPERTS * (MOE_ROWS - 1),
                       MOE_ROWS)
    idx, gate, cnt = _router(h, w_router, b_router)
    plan = _moe_plan(cnt, n_rows)
    tri = jnp.arange(tm)
    u = (tri[:, None] < tri[None, :]).astype(BF16)
    xs = _dispatch(h, idx, plan, u, n_rows)
    ys = _experts(xs, plan, w_up, b_up, w_down, b_down, layer)
    return _combine_ln(h, ys, idx, gate, plan, u, ln_g, ln_b)


def _ret_rope_tables(seq):
    inv_freq = 1.0 / (RET_ROPE_THETA ** jnp.linspace(0.0, 1.0, RET_QK_DIM // 2, dtype=F32))
    ang = jnp.arange(seq, dtype=F32)[:, None] * inv_freq[None, :]
    cos, sin = jnp.cos(ang), jnp.sin(ang)
    return jnp.concatenate([cos, cos], -1), jnp.concatenate([-sin, sin], -1)


def _partial_rope_tables(seq):
    inv_freq = 1.0 / (ROPE_THETA ** (jnp.arange(0, ROPE_DIM, 2, dtype=F32) / ROPE_DIM))
    ang = jnp.arange(seq, dtype=F32)[:, None] * inv_freq[None, :]
    cos, sin = jnp.cos(ang), jnp.sin(ang)
    pad = jnp.zeros((seq, DIFF_HEAD_DIM - ROPE_DIM), F32)
    c64 = jnp.concatenate([cos, cos, pad + 1.0], -1)
    s64 = jnp.concatenate([-sin, sin, pad], -1)
    return jnp.concatenate([c64, c64], -1), jnp.concatenate([s64, s64], -1)


def kernel(x, ret_w_in, ret_w_out, kv_w, diff_w_q, diff_lambda, diff_subln_g, diff_w_out,
           ln_attn_g, ln_attn_b, ln_ffn_g, ln_ffn_b, moe_w_router, moe_b_router,
           moe_w_up, moe_b_up, moe_w_down, moe_b_down):
    batch, seq, d = x.shape
    t = batch * seq
    h = x.reshape(t, d)
    ret_cos, ret_sin = _ret_rope_tables(seq)
    par_cos, par_sin = _partial_rope_tables(seq)
    qk_scale = jnp.concatenate([jnp.ones((1, d), F32), jnp.full((1, d), RET_QK_DIM ** -0.5, F32)], -1)
    k_sh = v_sh = None
    for l in range(DEPTH):
        if l < N_A_LAYERS:
            w_in = ret_w_in[l].astype(BF16)
            qk = _proj(h, w_in[:, :2 * d], BF16, rope="half", cos=ret_cos, sin=ret_sin,
                       scale=qk_scale, seq=seq)
            v = _proj(h, w_in[:, 2 * d:2 * d + RET_V_WIDTH], BF16)
            g = _proj(h, w_in[:, 2 * d + RET_V_WIDTH:], F32)
            a = _retention(qk, v, g, batch, seq)
            h = _outproj_ln(a, ret_w_out[l].astype(BF16), h, ln_attn_g[l], ln_attn_b[l])
        else:
            j = l - N_A_LAYERS
            lambda_init = 0.8 - 0.6 * math.exp(-0.3 * l)
            q = _proj(h, diff_w_q[j].astype(BF16), BF16, rope="partial", cos=par_cos, sin=par_sin,
                      scale=jnp.full((1, d), DIFF_HEAD_DIM ** -0.5 * math.log2(math.e), F32), seq=seq)
            a = _diff_attention(q, k_sh, v_sh, diff_lambda[j], diff_subln_g[j], lambda_init, batch, seq)
            h = _outproj_ln(a, diff_w_out[j].astype(BF16), h, ln_attn_g[l], ln_attn_b[l])
        h = _moe_ln(h, moe_w_router[l], moe_b_router[l], moe_w_up, moe_b_up, moe_w_down, moe_b_down,
                    l, ln_ffn_g[l], ln_ffn_b[l])
        if l == N_A_LAYERS - 1:
            kvw = kv_w.astype(BF16)
            k_sh = _proj(h, kvw[:, :d], BF16, rope="partial", cos=par_cos, sin=par_sin, seq=seq)
            v_sh = _transpose_values(_proj(h, kvw[:, d:], BF16), batch, seq)
    return h.reshape(batch, seq, d)
```

```python
import functools
import math

import jax
import jax.numpy as jnp
from jax import lax
from jax.experimental import pallas as pl
from jax.experimental.pallas import tpu as pltpu

F32 = jnp.float32
BF16 = jnp.bfloat16

D_MODEL = 1024
DEPTH = 4
N_A_LAYERS = DEPTH // 2
RET_HEADS = 8
RET_QK_DIM = D_MODEL // RET_HEADS
RET_V_DIM = 2 * RET_QK_DIM
RET_V_WIDTH = RET_HEADS * RET_V_DIM
RET_ROPE_THETA = 10000.0
DIFF_HEAD_DIM = 64
DIFF_HEADS = D_MODEL // (2 * DIFF_HEAD_DIM)
DIFF_V_DIM = 2 * DIFF_HEAD_DIM
ROPE_THETA = 500000.0
ROPE_DIM = DIFF_HEAD_DIM // 4
NEG_BIG = -1e30
N_EXPERTS = 32
TOP_K = 4
D_FF = D_MODEL
SWIGLU_ALPHA = 1.702
SWIGLU_LIMIT = 7.0
DN_ALPHA = (2 * DEPTH) ** 0.25
LN_EPS = 1e-5

LANES = 128
SUBLANES = 8
VMEM_LIMIT = 48 * 1024 * 1024
EXPERT_VMEM_LIMIT = 58 * 1024 * 1024

ROW_TILE = 512
RET_CHUNK = 256
ATTN_TILE = 512
ATTN_GROUP = 256
MOE_ROWS = 512
MOE_TILE = 512
ROW_ALIGN = SUBLANES
ROUTER_PAD = LANES


def _cparams(*sem):
    return pltpu.CompilerParams(dimension_semantics=sem, vmem_limit_bytes=VMEM_LIMIT)


def _round_up(x, m):
    return (x + m - 1) // m * m


def _layer_norm(y, g, b):
    mu = jnp.mean(y, -1, keepdims=True)
    d = y - mu
    var = jnp.mean(d * d, -1, keepdims=True)
    return d * lax.rsqrt(var + LN_EPS) * g + b


def _proj_kernel(x_ref, w_ref, c_ref, s_ref, scale_ref, o_ref, *, rope):
    acc = jnp.dot(x_ref[...].astype(BF16), w_ref[...], preferred_element_type=F32)
    tn = acc.shape[1]
    if rope is None:
        o_ref[...] = acc.astype(o_ref.dtype)
        return
    c = c_ref[...]
    s = s_ref[...]
    lane = lax.broadcasted_iota(jnp.int32, (1, LANES), 1)
    for g in range(tn // LANES):
        sl = slice(g * LANES, (g + 1) * LANES)
        xg = acc[:, sl]
        if rope == "half":
            y = xg * c + pltpu.roll(xg, LANES // 2, 1) * s
        else:
            half = ROPE_DIM // 2
            up = pltpu.roll(xg, LANES - half, 1)
            dn = pltpu.roll(xg, half, 1)
            partner = jnp.where((lane & half) == 0, up, dn)
            y = jnp.where((lane & (DIFF_HEAD_DIM - 1)) < ROPE_DIM, xg * c + partner * s, xg)
        o_ref[:, sl] = (y * scale_ref[:, sl]).astype(o_ref.dtype)


def _proj(x, w, out_dtype, *, rope=None, cos=None, sin=None, scale=None, seq=None, tn=1024):
    t, k = x.shape
    n = w.shape[1]
    tm = min(ROW_TILE, t)
    tn = min(tn, n)
    if cos is None:
        cos = jnp.zeros((tm, LANES), F32)
        sin = cos
        pos_blocks = 1
    else:
        pos_blocks = seq // tm
    if scale is None:
        scale = jnp.ones((1, n), F32)
    return pl.pallas_call(
        functools.partial(_proj_kernel, rope=rope),
        grid=(n // tn, t // tm),
        in_specs=[
            pl.BlockSpec((tm, k), lambda j, i: (i, 0)),
            pl.BlockSpec((k, tn), lambda j, i: (0, j)),
            pl.BlockSpec((tm, LANES), lambda j, i: (i % pos_blocks, 0)),
            pl.BlockSpec((tm, LANES), lambda j, i: (i % pos_blocks, 0)),
            pl.BlockSpec((1, tn), lambda j, i: (0, j)),
        ],
        out_specs=pl.BlockSpec((tm, tn), lambda j, i: (i, j)),
        out_shape=jax.ShapeDtypeStruct((t, n), out_dtype),
        compiler_params=_cparams("parallel", "parallel"),
        name="proj",
    )(x, w, cos, sin, scale)


def _outproj_ln_kernel(a_ref, w_ref, h_ref, g_ref, b_ref, o_ref):
    mix = jnp.dot(a_ref[...], w_ref[...], preferred_element_type=F32)
    o_ref[...] = _layer_norm(DN_ALPHA * h_ref[...] + mix, g_ref[...], b_ref[...])


def _outproj_ln(a, w, h, g, b):
    t, k = a.shape
    d = w.shape[1]
    tm = min(ROW_TILE, t)
    return pl.pallas_call(
        _outproj_ln_kernel,
        grid=(t // tm,),
        in_specs=[
            pl.BlockSpec((tm, k), lambda i: (i, 0)),
            pl.BlockSpec((k, d), lambda i: (0, 0)),
            pl.BlockSpec((tm, d), lambda i: (i, 0)),
            pl.BlockSpec((1, d), lambda i: (0, 0)),
            pl.BlockSpec((1, d), lambda i: (0, 0)),
        ],
        out_specs=pl.BlockSpec((tm, d), lambda i: (i, 0)),
        out_shape=jax.ShapeDtypeStruct((t, d), F32),
        compiler_params=_cparams("parallel"),
        name="outproj_ln",
    )(a, w, h, g.reshape(1, d), b.reshape(1, d))


def _retention_kernel(q_ref, k_ref, v_ref, g_ref, dm_ref, qd_ref, kd_ref, cd_ref, o_ref, state_ref):
    @pl.when(pl.program_id(1) == 0)
    def _():
        state_ref[...] = jnp.zeros_like(state_ref)

    dk, dv = RET_QK_DIM, RET_V_DIM
    heads = range(RET_HEADS)
    q = [q_ref[:, h * dk:(h + 1) * dk] for h in heads]
    k = [k_ref[:, h * dk:(h + 1) * dk] for h in heads]
    v = [v_ref[:, h * dv:(h + 1) * dv] for h in heads]
    s = [lax.dot_general(q[h], k[h], (((1,), (1,)), ((), ())), preferred_element_type=F32)
         for h in heads]
    state = [state_ref[h] for h in heads]
    cross = [jnp.dot(q[h], state[h].astype(BF16), preferred_element_type=F32) for h in heads]
    kdec = [(k[h].astype(F32) * kd_ref[h]).astype(BF16) for h in heads]
    kv = [lax.dot_general(kdec[h], v[h], (((0,), (0,)), ((), ())), preferred_element_type=F32)
          for h in heads]
    inner = [jnp.dot((s[h] * dm_ref[h]).astype(BF16), v[h], preferred_element_type=F32)
             for h in heads]
    for h in heads:
        state_ref[h] = cd_ref[h] * state[h] + kv[h]
        o = inner[h] + cross[h] * qd_ref[h]
        mu = jnp.mean(o, -1, keepdims=True)
        d = o - mu
        var = jnp.mean(d * d, -1, keepdims=True)
        gate = g_ref[:, h * dv:(h + 1) * dv]
        o_ref[:, h * dv:(h + 1) * dv] = (d * lax.rsqrt(var + LN_EPS)
                                         * (gate * jax.nn.sigmoid(gate))).astype(o_ref.dtype)


def _retention_tables(c):
    h = RET_HEADS
    log_gamma = jnp.log1p(-jnp.exp2(-5.0 - jnp.arange(h, dtype=F32)))
    idx = jnp.arange(c, dtype=F32)
    rel = idx[:, None] - idx[None, :]
    dmask = jnp.where(rel >= 0, jnp.exp(jnp.maximum(rel, 0.0) * log_gamma[:, None, None]), 0.0)
    k_decay = jnp.exp((c - 1 - idx) * log_gamma[:, None])[:, :, None]
    q_decay = jnp.exp((idx + 1) * log_gamma[:, None])[:, :, None]
    chunk_decay = jnp.exp(c * log_gamma)[:, None, None]
    return dmask, q_decay, k_decay, chunk_decay


def _retention(qk, v, g, batch, seq):
    t = qk.shape[0]
    c = min(RET_CHUNK, seq)
    n = seq // c
    h, dk, dv = RET_HEADS, RET_QK_DIM, RET_V_DIM
    dmask, q_decay, k_decay, chunk_decay = _retention_tables(c)
    row = lambda b, nn: b * n + nn
    whole = lambda b, nn: (0, 0, 0)
    return pl.pallas_call(
        _retention_kernel,
        grid=(batch, n),
        in_specs=[
            pl.BlockSpec((c, h * dk), lambda b, nn: (row(b, nn), 0)),
            pl.BlockSpec((c, h * dk), lambda b, nn: (row(b, nn), 1)),
            pl.BlockSpec((c, h * dv), lambda b, nn: (row(b, nn), 0)),
            pl.BlockSpec((c, h * dv), lambda b, nn: (row(b, nn), 0)),
            pl.BlockSpec((h, c, c), whole),
            pl.BlockSpec((h, c, 1), whole),
            pl.BlockSpec((h, c, 1), whole),
            pl.BlockSpec((h, 1, 1), whole),
        ],
        out_specs=pl.BlockSpec((c, h * dv), lambda b, nn: (row(b, nn), 0)),
        out_shape=jax.ShapeDtypeStruct((t, h * dv), BF16),
        scratch_shapes=[pltpu.VMEM((h, dk, dv), F32)],
        compiler_params=_cparams("parallel", "arbitrary"),
        name="retention",
    )(qk, qk, v, g, dmask, q_decay, k_decay, chunk_decay)


def _diff_attn_kernel(q_ref, k_ref, vt_ref, lam_ref, sg_ref, o_ref, qm_ref, m_ref, l_ref, acc_ref,
                      s0_ref, s1_ref, *, lambda_init):
    i = pl.program_id(2)
    tq = q_ref.shape[0]
    tk = vt_ref.shape[2]
    gw = min(ATTN_GROUP, tq)

    q = q_ref[...]
    lane = lax.broadcasted_iota(jnp.int32, q.shape, 1)
    zero = jnp.zeros_like(q)
    qm_ref[:tq, :] = jnp.where(lane < DIFF_HEAD_DIM, q, zero)
    qm_ref[tq:, :] = jnp.where(lane >= DIFF_HEAD_DIM, q, zero)
    m_ref[...] = jnp.full_like(m_ref, NEG_BIG)
    l_ref[...] = jnp.zeros_like(l_ref)
    acc_ref[...] = jnp.zeros_like(acc_ref)

    groups = [slice(g * gw, (g + 1) * gw) for g in range(2 * tq // gw)]

    def score_tile(s_ref, j):
        kt = k_ref[pl.ds(pl.multiple_of(j * tk, tk), tk), :]
        for cols in groups:
            s_ref[:, cols] = lax.dot_general(kt, qm_ref[cols, :], (((1,), (1,)), ((), ())),
                                             preferred_element_type=F32)

    def consume(s_ref, j, masked):
        vt = vt_ref[j]
        for g, cols in enumerate(groups):
            s = s_ref[:, cols]
            if masked:
                kpos = lax.broadcasted_iota(jnp.int32, s.shape, 0)
                qpos = (lax.broadcasted_iota(jnp.int32, s.shape, 1) + g * gw) & (tq - 1)
                s = jnp.where(kpos <= qpos, s, NEG_BIG)
            m_old = m_ref[:, cols]
            m_new = jnp.maximum(m_old, jnp.max(s, 0, keepdims=True))
            alpha = jnp.exp2(m_old - m_new)
            p = jnp.exp2(s - m_new)
            l_ref[:, cols] = alpha * l_ref[:, cols] + jnp.sum(p, 0, keepdims=True)
            acc_ref[:, cols] = alpha * acc_ref[:, cols] + jnp.dot(vt, p.astype(BF16),
                                                                  preferred_element_type=F32)
            m_ref[:, cols] = m_new

    score_tile(s0_ref, 0)

    def pair(jj, carry):
        j = 2 * jj
        score_tile(s1_ref, j + 1)
        consume(s0_ref, j, False)
        score_tile(s0_ref, j + 2)
        consume(s1_ref, j + 1, False)
        return carry

    lax.fori_loop(0, i // 2, pair, 0)

    @pl.when(i % 2 == 0)
    def _():
        consume(s0_ref, i, True)

    @pl.when(i % 2 == 1)
    def _():
        score_tile(s1_ref, i)
        consume(s0_ref, i - 1, False)
        consume(s1_ref, i, True)

    lam = lam_ref[...]
    lam_full = (jnp.exp(jnp.sum(lam[0:1] * lam[1:2], -1, keepdims=True))
                - jnp.exp(jnp.sum(lam[2:3] * lam[3:4], -1, keepdims=True)) + lambda_init)
    o1 = acc_ref[:, :tq] / l_ref[:, :tq]
    o2 = acc_ref[:, tq:] / l_ref[:, tq:]
    o = o1 - lam_full * o2
    o = o * lax.rsqrt(jnp.mean(o * o, 0, keepdims=True) + LN_EPS) * sg_ref[...] * (1.0 - lambda_init)
    o_ref[...] = o.T.astype(o_ref.dtype)


def _diff_attention(q, k, vt, lam, subln_g, lambda_init, batch, seq):
    t, d = q.shape
    tq = min(ATTN_TILE, seq)
    nq = seq // tq
    hd = 2 * DIFF_HEAD_DIM
    return pl.pallas_call(
        functools.partial(_diff_attn_kernel, lambda_init=lambda_init),
        grid=(batch, DIFF_HEADS, nq),
        in_specs=[
            pl.BlockSpec((tq, hd), lambda b, h, i: (b * nq + i, h)),
            pl.BlockSpec((seq, hd), lambda b, h, i: (b, h)),
            pl.BlockSpec((nq, hd, tq), lambda b, h, i: (b * DIFF_HEADS + h, 0, 0)),
            pl.BlockSpec((4, DIFF_HEAD_DIM), lambda b, h, i: (0, 0)),
            pl.BlockSpec((hd, 1), lambda b, h, i: (0, 0)),
        ],
        out_specs=pl.BlockSpec((tq, hd), lambda b, h, i: (b * nq + i, h)),
        out_shape=jax.ShapeDtypeStruct((t, d), BF16),
        scratch_shapes=[
            pltpu.VMEM((2 * tq, hd), BF16),
            pltpu.VMEM((1, 2 * tq), F32),
            pltpu.VMEM((1, 2 * tq), F32),
            pltpu.VMEM((hd, 2 * tq), F32),
            pltpu.VMEM((tq, 2 * tq), F32),
            pltpu.VMEM((tq, 2 * tq), F32),
        ],
        compiler_params=_cparams("parallel", "parallel", "arbitrary"),
        name="diff_attn",
    )(q, k, vt, lam, subln_g.reshape(hd, 1))


def _transpose_values(v, batch, seq):
    tk = min(ATTN_TILE, seq)
    nk = seq // tk
    hd = 2 * DIFF_HEAD_DIM
    v5 = v.reshape(batch, nk, tk, DIFF_HEADS, hd).transpose(0, 3, 1, 4, 2)
    return v5.reshape(batch * DIFF_HEADS * nk, hd, tk)


def _router_kernel(h_ref, w_ref, b_ref, idx_ref, gate_ref, cnt_ref):
    logits = jnp.dot(h_ref[...], w_ref[...], preferred_element_type=F32,
                     precision=lax.Precision.HIGHEST) + b_ref[...]
    lt = logits.T
    tm = lt.shape[1]
    row = lax.broadcasted_iota(jnp.int32, lt.shape, 0)
    cur = jnp.where(row < N_EXPERTS, lt, -jnp.inf)
    vals, idxs = [], []
    for _ in range(TOP_K):
        m = jnp.max(cur, 0, keepdims=True)
        sel = jnp.min(jnp.where(cur == m, row, ROUTER_PAD), 0, keepdims=True)
        vals.append(m)
        idxs.append(sel)
        cur = jnp.where(row == sel, -jnp.inf, cur)
    es = [jnp.exp(vk - vals[0]) for vk in vals]
    denom = es[0] + es[1] + es[2] + es[3]
    row8 = lax.broadcasted_iota(jnp.int32, (SUBLANES, tm), 0)
    idx_out = jnp.full((SUBLANES, tm), -1, jnp.int32)
    gate_out = jnp.zeros((SUBLANES, tm), F32)
    chosen = jnp.zeros(lt.shape, F32)
    for kk in range(TOP_K):
        idx_out = jnp.where(row8 == kk, idxs[kk], idx_out)
        gate_out = jnp.where(row8 == kk, es[kk] / denom, gate_out)
        chosen = jnp.where(row == idxs[kk], 1.0, chosen)
    idx_ref[...] = idx_out
    gate_ref[...] = gate_out
    ones = jnp.ones((SUBLANES, tm), BF16)
    cnt = lax.dot_general(ones, chosen.astype(BF16), (((1,), (1,)), ((), ())),
                          preferred_element_type=F32)
    cnt_ref[0] = cnt.astype(jnp.int32)


def _router(h, w_router, b_router):
    t, d = h.shape
    tm = min(MOE_TILE, t)
    nt = t // tm
    w = jnp.zeros((d, ROUTER_PAD), F32).at[:, :N_EXPERTS].set(w_router)
    b = jnp.zeros((1, ROUTER_PAD), F32).at[0, :N_EXPERTS].set(b_router)
    idx, gate, cnt = pl.pallas_call(
        _router_kernel,
        grid=(nt,),
        in_specs=[
            pl.BlockSpec((tm, d), lambda i: (i, 0)),
            pl.BlockSpec((d, ROUTER_PAD), lambda i: (0, 0)),
            pl.BlockSpec((1, ROUTER_PAD), lambda i: (0, 0)),
        ],
        out_specs=[
            pl.BlockSpec((SUBLANES, tm), lambda i: (0, i)),
            pl.BlockSpec((SUBLANES, tm), lambda i: (0, i)),
            pl.BlockSpec((1, SUBLANES, ROUTER_PAD), lambda i: (i, 0, 0)),
        ],
        out_shape=[
            jax.ShapeDtypeStruct((SUBLANES, t), jnp.int32),
            jax.ShapeDtypeStruct((SUBLANES, t), F32),
            jax.ShapeDtypeStruct((nt, SUBLANES, ROUTER_PAD), jnp.int32),
        ],
        compiler_params=_cparams("parallel"),
        name="router",
    )(h, w, b)
    return idx, gate, cnt[:, 0, :N_EXPERTS]


def _moe_plan(cnt, n_rows):
    a, blk = ROW_ALIGN, MOE_ROWS
    cnt_a = _round_up(cnt, a)
    rows_e = jnp.sum(cnt_a, 0)
    cap_e = _round_up(rows_e, blk)
    seg_end = jnp.cumsum(cap_e)
    seg_start = seg_end - cap_e
    gbase = seg_start[None, :] + jnp.cumsum(cnt_a, 0) - cnt_a
    loff = jnp.cumsum(cnt_a, 1) - cnt_a
    nch = cnt_a // a
    nb = n_rows // blk
    n_used = seg_end[-1] // blk
    block_row = jnp.minimum(jnp.arange(nb, dtype=jnp.int32), n_used - 1) * blk
    block_e = jnp.minimum(jnp.sum(seg_end[None, :] <= block_row[:, None], 1), N_EXPERTS - 1)
    loffv = jnp.zeros((cnt.shape[0], ROUTER_PAD, 1), F32).at[:, :N_EXPERTS, 0].set(loff.astype(F32))
    i32 = lambda x: x.astype(jnp.int32)
    return dict(nch=i32(nch), gbase=i32(gbase), loff=i32(loff), ntot=i32(jnp.sum(nch, 1)),
                pad_start=i32(seg_start + rows_e), pad_n=i32((cap_e - rows_e) // a),
                block_e=i32(block_e), n_used=i32(n_used).reshape(1), loffv=loffv)


def _local_positions(idx, loffv, u):
    tm = idx.shape[1]
    row = lax.broadcasted_iota(jnp.int32, (ROUTER_PAD, tm), 0)
    hits = [row == idx[kk:kk + 1, :] for kk in range(TOP_K)]
    chosen = jnp.zeros((ROUTER_PAD, tm), F32)
    for hit in hits:
        chosen = jnp.where(hit, 1.0, chosen)
    before = jnp.dot(chosen.astype(BF16), u, preferred_element_type=F32)
    pos = before + loffv
    return [jnp.sum(jnp.where(hit, pos, 0.0), 0, keepdims=True).astype(jnp.int32) for hit in hits]


def _selection_matrix(js, vals, n_local, tm):
    rr = lax.broadcasted_iota(jnp.int32, (n_local, tm), 0)
    out = jnp.zeros((n_local, tm), F32)
    for j, v in zip(js, vals):
        out = jnp.where(rr == j, v, out)
    return out


def _run_copies(nch_s, src_s, dst_s, i, copy):
    def per_expert(e, carry):
        src0 = src_s[i, e]
        dst0 = dst_s[i, e]

        def per_chunk(c, carry2):
            copy(pl.multiple_of(src0 + c * ROW_ALIGN, ROW_ALIGN),
                 pl.multiple_of(dst0 + c * ROW_ALIGN, ROW_ALIGN)).start()
            return carry2

        return lax.fori_loop(0, nch_s[i, e], per_chunk, carry)

    lax.fori_loop(0, N_EXPERTS, per_expert, 0)


def _drain(n, copy):
    def body(c, carry):
        copy(0, 0).wait()
        return carry

    lax.fori_loop(0, n, body, 0)


def _dispatch_kernel(nch_s, gbase_s, loff_s, ntot_s, pstart_s, pn_s, nu_s,
                     h_ref, idx_ref, loffv_ref, u_ref, xs_hbm, loc_ref, zero_ref, sem, zsem):
    i = pl.program_id(0)
    n_local, tm = loc_ref.shape[0], h_ref.shape[0]
    js = _local_positions(idx_ref[...], loffv_ref[0], u_ref[...])
    perm = _selection_matrix(js, [1.0] * TOP_K, n_local, tm).astype(BF16)
    loc_ref[...] = jnp.dot(perm, h_ref[...].astype(BF16), preferred_element_type=F32)

    def copy(src_row, dst_row):
        return pltpu.make_async_copy(loc_ref.at[pl.ds(src_row, ROW_ALIGN)],
                                     xs_hbm.at[pl.ds(dst_row, ROW_ALIGN)], sem)

    def zero_copy(src_row, dst_row):
        return pltpu.make_async_copy(zero_ref.at[pl.ds(src_row, ROW_ALIGN)],
                                     xs_hbm.at[pl.ds(dst_row, ROW_ALIGN)], zsem)

    def zero_block_copy(block):
        return pltpu.make_async_copy(
            zero_ref, xs_hbm.at[pl.ds(pl.multiple_of(block * MOE_ROWS, MOE_ROWS), MOE_ROWS)], zsem)

    _run_copies(nch_s, loff_s, gbase_s, i, copy)

    @pl.when(i == pl.num_programs(0) - 1)
    def _():
        zero_ref[...] = jnp.zeros_like(zero_ref)

        def per_expert(e, total):
            def per_chunk(c, carry):
                zero_copy(0, pl.multiple_of(pstart_s[e] + c * ROW_ALIGN, ROW_ALIGN)).start()
                return carry

            lax.fori_loop(0, pn_s[e], per_chunk, 0)
            return total + pn_s[e]

        total = lax.fori_loop(0, N_EXPERTS, per_expert, 0)
        _drain(total, zero_copy)
        n_blocks = xs_hbm.shape[0] // MOE_ROWS

        def start_block(blk, carry):
            zero_block_copy(blk).start()
            return carry

        def wait_block(blk, carry):
            zero_block_copy(blk).wait()
            return carry

        lax.fori_loop(nu_s[0], n_blocks, start_block, 0)
        lax.fori_loop(nu_s[0], n_blocks, wait_block, 0)

    _drain(ntot_s[i], copy)


def _dispatch(h, idx, plan, u, n_rows):
    t, d = h.shape
    tm = min(MOE_TILE, t)
    n_local = _round_up(TOP_K * tm + N_EXPERTS * (ROW_ALIGN - 1), 2 * LANES)
    grid_spec = pltpu.PrefetchScalarGridSpec(
        num_scalar_prefetch=7,
        grid=(t // tm,),
        in_specs=[
            pl.BlockSpec((tm, d), lambda i, *_: (i, 0)),
            pl.BlockSpec((SUBLANES, tm), lambda i, *_: (0, i)),
            pl.BlockSpec((1, ROUTER_PAD, 1), lambda i, *_: (i, 0, 0)),
            pl.BlockSpec((tm, tm), lambda i, *_: (0, 0)),
        ],
        out_specs=pl.BlockSpec(memory_space=pl.ANY),
        scratch_shapes=[
            pltpu.VMEM((n_local, d), F32),
            pltpu.VMEM((MOE_ROWS, d), F32),
            pltpu.SemaphoreType.DMA,
            pltpu.SemaphoreType.DMA,
        ],
    )
    return pl.pallas_call(
        _dispatch_kernel,
        grid_spec=grid_spec,
        out_shape=jax.ShapeDtypeStruct((n_rows, d), F32),
        compiler_params=_cparams("arbitrary"),
        name="dispatch",
    )(plan["nch"], plan["gbase"], plan["loff"], plan["ntot"], plan["pad_start"], plan["pad_n"],
      plan["n_used"], h, idx, plan["loffv"], u)


def _expert_kernel(be_ref, nu_ref, x_ref, wu_ref, bu_ref, wd_ref, bd_ref, o_ref, wu_bf, wd_bf):
    i = pl.program_id(0)

    @pl.when(jnp.logical_or(i == 0, be_ref[i] != be_ref[jnp.maximum(i - 1, 0)]))
    def _():
        wu_bf[...] = wu_ref[0, 0].astype(BF16)
        wd_bf[...] = wd_ref[0, 0].astype(BF16)

    @pl.when(i < nu_ref[0])
    def _():
        hdn = jnp.dot(x_ref[...].astype(BF16), wu_bf[...], preferred_element_type=F32) + bu_ref[0, 0]
        glu = jnp.minimum(hdn[:, :D_FF], SWIGLU_LIMIT)
        lin = jnp.clip(hdn[:, D_FF:], -SWIGLU_LIMIT, SWIGLU_LIMIT)
        act = glu * jax.nn.sigmoid(SWIGLU_ALPHA * glu) * (lin + 1.0)
        o_ref[...] = jnp.dot(act.astype(BF16), wd_bf[...], preferred_element_type=F32) + bd_ref[0, 0]

    @pl.when(i >= nu_ref[0])
    def _():
        o_ref[...] = jnp.zeros_like(o_ref)


def _experts(xs, plan, w_up, b_up, w_down, b_down, layer):
    n_rows, d = xs.shape
    depth, e = w_up.shape[:2]
    used = lambda i, nu: jnp.maximum(jnp.minimum(i, nu[0] - 1), 0)
    grid_spec = pltpu.PrefetchScalarGridSpec(
        num_scalar_prefetch=2,
        grid=(n_rows // MOE_ROWS,),
        in_specs=[
            pl.BlockSpec((MOE_ROWS, d), lambda i, be, nu: (used(i, nu), 0)),
            pl.BlockSpec((1, 1, d, 2 * D_FF), lambda i, be, nu: (layer, be[i], 0, 0)),
            pl.BlockSpec((1, 1, 1, 2 * D_FF), lambda i, be, nu: (layer, be[i], 0, 0)),
            pl.BlockSpec((1, 1, D_FF, d), lambda i, be, nu: (layer, be[i], 0, 0)),
            pl.BlockSpec((1, 1, 1, d), lambda i, be, nu: (layer, be[i], 0, 0)),
        ],
        out_specs=pl.BlockSpec((MOE_ROWS, d), lambda i, be, nu: (i, 0)),
        scratch_shapes=[pltpu.VMEM((d, 2 * D_FF), BF16), pltpu.VMEM((D_FF, d), BF16)],
    )
    return pl.pallas_call(
        _expert_kernel,
        grid_spec=grid_spec,
        out_shape=jax.ShapeDtypeStruct((n_rows, d), F32),
        compiler_params=pltpu.CompilerParams(dimension_semantics=("arbitrary",),
                                             vmem_limit_bytes=EXPERT_VMEM_LIMIT),
        name="experts",
    )(plan["block_e"], plan["n_used"], xs, w_up, b_up.reshape(depth, e, 1, 2 * D_FF), w_down,
      b_down.reshape(depth, e, 1, d))


def _combine_kernel(nch_s, gbase_s, loff_s, ntot_s,
                    h_ref, idx_ref, gate_ref, loffv_ref, u_ref, g_ref, b_ref, ys_hbm,
                    o_ref, loc_ref, sem):
    i = pl.program_id(0)
    n_local, tm = loc_ref.shape[0], h_ref.shape[0]

    @pl.when(i == 0)
    def _():
        loc_ref[...] = jnp.zeros_like(loc_ref)

    def copy(dst_row, src_row):
        return pltpu.make_async_copy(ys_hbm.at[pl.ds(src_row, ROW_ALIGN)],
                                     loc_ref.at[pl.ds(dst_row, ROW_ALIGN)], sem)

    _run_copies(nch_s, loff_s, gbase_s, i, copy)
    js = _local_positions(idx_ref[...], loffv_ref[0], u_ref[...])
    gates = gate_ref[...]
    gmat = _selection_matrix(js, [gates[kk:kk + 1, :] for kk in range(TOP_K)], n_local, tm)
    _drain(ntot_s[i], copy)
    ffn = lax.dot_general(gmat.astype(BF16), loc_ref[...].astype(BF16), (((0,), (0,)), ((), ())),
                          preferred_element_type=F32)
    o_ref[...] = _layer_norm(DN_ALPHA * h_ref[...] + ffn, g_ref[...], b_ref[...])


def _combine_ln(h, ys, idx, gate, plan, u, g, b):
    t, d = h.shape
    tm = min(MOE_TILE, t)
    n_local = _round_up(TOP_K * tm + N_EXPERTS * (ROW_ALIGN - 1), 2 * LANES)
    grid_spec = pltpu.PrefetchScalarGridSpec(
        num_scalar_prefetch=4,
        grid=(t // tm,),
        in_specs=[
            pl.BlockSpec((tm, d), lambda i, *_: (i, 0)),
            pl.BlockSpec((SUBLANES, tm), lambda i, *_: (0, i)),
            pl.BlockSpec((SUBLANES, tm), lambda i, *_: (0, i)),
            pl.BlockSpec((1, ROUTER_PAD, 1), lambda i, *_: (i, 0, 0)),
            pl.BlockSpec((tm, tm), lambda i, *_: (0, 0)),
            pl.BlockSpec((1, d), lambda i, *_: (0, 0)),
            pl.BlockSpec((1, d), lambda i, *_: (0, 0)),
            pl.BlockSpec(memory_space=pl.ANY),
        ],
        out_specs=pl.BlockSpec((tm, d), lambda i, *_: (i, 0)),
        scratch_shapes=[pltpu.VMEM((n_local, d), F32), pltpu.SemaphoreType.DMA],
    )
    return pl.pallas_call(
        _combine_kernel,
        grid_spec=grid_spec,
        out_shape=jax.ShapeDtypeStruct((t, d), F32),
        compiler_params=_cparams("arbitrary"),
        name="combine_ln",
    )(plan["nch"], plan["gbase"], plan["loff"], plan["ntot"],
      h, idx, gate, plan["loffv"], u, g.reshape(1, d), b.reshape(1, d), ys)


def _moe_ln(h, w_router, b_router, w_up, b_up, w_down, b_down, layer, ln_g, ln_b):
    t, d = h.shape
    tm = min(MOE_TILE, t)
    nt = t // tm
    n_rows = _round_up(t * TOP_K + nt * N_EXPERTS * (ROW_ALIGN - 1) + N_EXPERTS * (MOE_ROWS - 1),
                       MOE_ROWS)
    idx, gate, cnt = _router(h, w_router, b_router)
    plan = _moe_plan(cnt, n_rows)
    tri = jnp.arange(tm)
    u = (tri[:, None] < tri[None, :]).astype(BF16)
    xs = _dispatch(h, idx, plan, u, n_rows)
    ys = _experts(xs, plan, w_up, b_up, w_down, b_down, layer)
    return _combine_ln(h, ys, idx, gate, plan, u, ln_g, ln_b)


def _ret_rope_tables(seq):
    inv_freq = 1.0 / (RET_ROPE_THETA ** jnp.linspace(0.0, 1.0, RET_QK_DIM // 2, dtype=F32))
    ang = jnp.arange(seq, dtype=F32)[:, None] * inv_freq[None, :]
    cos, sin = jnp.cos(ang), jnp.sin(ang)
    return jnp.concatenate([cos, cos], -1), jnp.concatenate([-sin, sin], -1)


def _partial_rope_tables(seq):
    inv_freq = 1.0 / (ROPE_THETA ** (jnp.arange(0, ROPE_DIM, 2, dtype=F32) / ROPE_DIM))
    ang = jnp.arange(seq, dtype=F32)[:, None] * inv_freq[None, :]
    cos, sin = jnp.cos(ang), jnp.sin(ang)
    pad = jnp.zeros((seq, DIFF_HEAD_DIM - ROPE_DIM), F32)
    c64 = jnp.concatenate([cos, cos, pad + 1.0], -1)
    s64 = jnp.concatenate([-sin, sin, pad], -1)
    return jnp.concatenate([c64, c64], -1), jnp.concatenate([s64, s64], -1)


def kernel(x, ret_w_in, ret_w_out, kv_w, diff_w_q, diff_lambda, diff_subln_g, diff_w_out,
           ln_attn_g, ln_attn_b, ln_ffn_g, ln_ffn_b, moe_w_router, moe_b_router,
           moe_w_up, moe_b_up, moe_w_down, moe_b_down):
    batch, seq, d = x.shape
    t = batch * seq
    h = x.reshape(t, d)
    ret_cos, ret_sin = _ret_rope_tables(seq)
    par_cos, par_sin = _partial_rope_tables(seq)
    qk_scale = jnp.concatenate([jnp.ones((1, d), F32), jnp.full((1, d), RET_QK_DIM ** -0.5, F32)], -1)
    k_sh = v_sh = None
    for l in range(DEPTH):
        if l < N_A_LAYERS:
            w_in = ret_w_in[l].astype(BF16)
            qk = _proj(h, w_in[:, :2 * d], BF16, rope="half", cos=ret_cos, sin=ret_sin,
                       scale=qk_scale, seq=seq)
            v = _proj(h, w_in[:, 2 * d:2 * d + RET_V_WIDTH], BF16)
            g = _proj(h, w_in[:, 2 * d + RET_V_WIDTH:], F32)
            a = _retention(qk, v, g, batch, seq)
            h = _outproj_ln(a, ret_w_out[l].astype(BF16), h, ln_attn_g[l], ln_attn_b[l])
        else:
            j = l - N_A_LAYERS
            lambda_init = 0.8 - 0.6 * math.exp(-0.3 * l)
            q = _proj(h, diff_w_q[j].astype(BF16), BF16, rope="partial", cos=par_cos, sin=par_sin,
                      scale=jnp.full((1, d), DIFF_HEAD_DIM ** -0.5 * math.log2(math.e), F32), seq=seq)
            a = _diff_attention(q, k_sh, v_sh, diff_lambda[j], diff_subln_g[j], lambda_init, batch, seq)
            h = _outproj_ln(a, diff_w_out[j].astype(BF16), h, ln_attn_g[l], ln_attn_b[l])
        h = _moe_ln(h, moe_w_router[l], moe_b_router[l], moe_w_up, moe_b_up, moe_w_down, moe_b_down,
                    l, ln_ffn_g[l], ln_ffn_b[l])
        if l == N_A_LAYERS - 1:
            kvw = kv_w.astype(BF16)
            k_sh = _proj(h, kvw[:, :d], BF16, rope="partial", cos=par_cos, sin=par_sin, seq=seq)
            v_sh = _transpose_values(_proj(h, kvw[:, d:], BF16), batch, seq)
    return h.reshape(batch, seq, d)
```

```python
import functools
import math

import jax
import jax.numpy as jnp
from jax import lax
from jax.experimental import pallas as pl
from jax.experimental.pallas import tpu as pltpu

F32 = jnp.float32
BF16 = jnp.bfloat16

D_MODEL = 1024
DEPTH = 4
N_A_LAYERS = DEPTH // 2
RET_HEADS = 8
RET_QK_DIM = D_MODEL // RET_HEADS
RET_V_DIM = 2 * RET_QK_DIM
RET_V_WIDTH = RET_HEADS * RET_V_DIM
RET_ROPE_THETA = 10000.0
DIFF_HEAD_DIM = 64
DIFF_HEADS = D_MODEL // (2 * DIFF_HEAD_DIM)
DIFF_V_DIM = 2 * DIFF_HEAD_DIM
ROPE_THETA = 500000.0
ROPE_DIM = DIFF_HEAD_DIM // 4
NEG_BIG = -1e30
N_EXPERTS = 32
TOP_K = 4
D_FF = D_MODEL
SWIGLU_ALPHA = 1.702
SWIGLU_LIMIT = 7.0
DN_ALPHA = (2 * DEPTH) ** 0.25
LN_EPS = 1e-5

LANES = 128
SUBLANES = 8
VMEM_LIMIT = 48 * 1024 * 1024
EXPERT_VMEM_LIMIT = 58 * 1024 * 1024

ROW_TILE = 512
PROJ_TILE = 1024
RET_CHUNK = 256
ATTN_TILE = 512
ATTN_GROUP = 256
MOE_ROWS = 512
MOE_TILE = 256
ROW_ALIGN = SUBLANES
ROUTER_PAD = LANES


def _cparams(*sem):
    return pltpu.CompilerParams(dimension_semantics=sem, vmem_limit_bytes=VMEM_LIMIT)


def _round_up(x, m):
    return (x + m - 1) // m * m


def _layer_norm(y, g, b):
    mu = jnp.mean(y, -1, keepdims=True)
    d = y - mu
    var = jnp.mean(d * d, -1, keepdims=True)
    return d * lax.rsqrt(var + LN_EPS) * g + b


def _proj_kernel(x_ref, w_ref, c_ref, s_ref, scale_ref, o_ref, *, rope):
    acc = jnp.dot(x_ref[...].astype(BF16), w_ref[...], preferred_element_type=F32)
    tn = acc.shape[1]
    if rope is None:
        o_ref[...] = acc.astype(o_ref.dtype)
        return
    c = c_ref[...]
    s = s_ref[...]
    lane = lax.broadcasted_iota(jnp.int32, (1, LANES), 1)
    for g in range(tn // LANES):
        sl = slice(g * LANES, (g + 1) * LANES)
        xg = acc[:, sl]
        if rope == "half":
            y = xg * c + pltpu.roll(xg, LANES // 2, 1) * s
        else:
            half = ROPE_DIM // 2
            up = pltpu.roll(xg, LANES - half, 1)
            dn = pltpu.roll(xg, half, 1)
            partner = jnp.where((lane & half) == 0, up, dn)
            y = jnp.where((lane & (DIFF_HEAD_DIM - 1)) < ROPE_DIM, xg * c + partner * s, xg)
        o_ref[:, sl] = (y * scale_ref[:, sl]).astype(o_ref.dtype)


def _proj(x, w, out_dtype, *, rope=None, cos=None, sin=None, scale=None, seq=None, tn=1024):
    t, k = x.shape
    n = w.shape[1]
    tm = min(PROJ_TILE, t, seq or t)
    tn = min(tn, n)
    if cos is None:
        cos = jnp.zeros((tm, LANES), F32)
        sin = cos
        pos_blocks = 1
    else:
        pos_blocks = seq // tm
    if scale is None:
        scale = jnp.ones((1, n), F32)
    return pl.pallas_call(
        functools.partial(_proj_kernel, rope=rope),
        grid=(n // tn, t // tm),
        in_specs=[
            pl.BlockSpec((tm, k), lambda j, i: (i, 0)),
            pl.BlockSpec((k, tn), lambda j, i: (0, j)),
            pl.BlockSpec((tm, LANES), lambda j, i: (i % pos_blocks, 0)),
            pl.BlockSpec((tm, LANES), lambda j, i: (i % pos_blocks, 0)),
            pl.BlockSpec((1, tn), lambda j, i: (0, j)),
        ],
        out_specs=pl.BlockSpec((tm, tn), lambda j, i: (i, j)),
        out_shape=jax.ShapeDtypeStruct((t, n), out_dtype),
        compiler_params=_cparams("parallel", "parallel"),
        name="proj",
    )(x, w, cos, sin, scale)


def _outproj_ln_kernel(a_ref, w_ref, h_ref, g_ref, b_ref, o_ref):
    mix = jnp.dot(a_ref[...], w_ref[...], preferred_element_type=F32)
    o_ref[...] = _layer_norm(DN_ALPHA * h_ref[...] + mix, g_ref[...], b_ref[...])


def _outproj_ln(a, w, h, g, b):
    t, k = a.shape
    d = w.shape[1]
    tm = min(ROW_TILE, t)
    return pl.pallas_call(
        _outproj_ln_kernel,
        grid=(t // tm,),
        in_specs=[
            pl.BlockSpec((tm, k), lambda i: (i, 0)),
            pl.BlockSpec((k, d), lambda i: (0, 0)),
            pl.BlockSpec((tm, d), lambda i: (i, 0)),
            pl.BlockSpec((1, d), lambda i: (0, 0)),
            pl.BlockSpec((1, d), lambda i: (0, 0)),
        ],
        out_specs=pl.BlockSpec((tm, d), lambda i: (i, 0)),
        out_shape=jax.ShapeDtypeStruct((t, d), F32),
        compiler_params=_cparams("parallel"),
        name="outproj_ln",
    )(a, w, h, g.reshape(1, d), b.reshape(1, d))


def _retention_kernel(q_ref, k_ref, v_ref, g_ref, dm_ref, qd_ref, kd_ref, cd_ref, o_ref, state_ref):
    @pl.when(pl.program_id(1) == 0)
    def _():
        state_ref[...] = jnp.zeros_like(state_ref)

    dk, dv = RET_QK_DIM, RET_V_DIM
    heads = range(RET_HEADS)
    q = [q_ref[:, h * dk:(h + 1) * dk] for h in heads]
    k = [k_ref[:, h * dk:(h + 1) * dk] for h in heads]
    v = [v_ref[:, h * dv:(h + 1) * dv] for h in heads]
    s = [lax.dot_general(q[h], k[h], (((1,), (1,)), ((), ())), preferred_element_type=F32)
         for h in heads]
    state = [state_ref[h] for h in heads]
    cross = [jnp.dot(q[h], state[h].astype(BF16), preferred_element_type=F32) for h in heads]
    kdec = [(k[h].astype(F32) * kd_ref[h]).astype(BF16) for h in heads]
    kv = [lax.dot_general(kdec[h], v[h], (((0,), (0,)), ((), ())), preferred_element_type=F32)
          for h in heads]
    inner = [jnp.dot((s[h] * dm_ref[h]).astype(BF16), v[h], preferred_element_type=F32)
             for h in heads]
    for h in heads:
        state_ref[h] = cd_ref[h] * state[h] + kv[h]
        o = inner[h] + cross[h] * qd_ref[h]
        mu = jnp.mean(o, -1, keepdims=True)
        d = o - mu
        var = jnp.mean(d * d, -1, keepdims=True)
        gate = g_ref[:, h * dv:(h + 1) * dv]
        o_ref[:, h * dv:(h + 1) * dv] = (d * lax.rsqrt(var + LN_EPS)
                                         * (gate * jax.nn.sigmoid(gate))).astype(o_ref.dtype)


def _retention_tables(c):
    h = RET_HEADS
    log_gamma = jnp.log1p(-jnp.exp2(-5.0 - jnp.arange(h, dtype=F32)))
    idx = jnp.arange(c, dtype=F32)
    rel = idx[:, None] - idx[None, :]
    dmask = jnp.where(rel >= 0, jnp.exp(jnp.maximum(rel, 0.0) * log_gamma[:, None, None]), 0.0)
    k_decay = jnp.exp((c - 1 - idx) * log_gamma[:, None])[:, :, None]
    q_decay = jnp.exp((idx + 1) * log_gamma[:, None])[:, :, None]
    chunk_decay = jnp.exp(c * log_gamma)[:, None, None]
    return dmask, q_decay, k_decay, chunk_decay


def _retention(qk, v, g, batch, seq):
    t = qk.shape[0]
    c = min(RET_CHUNK, seq)
    n = seq // c
    h, dk, dv = RET_HEADS, RET_QK_DIM, RET_V_DIM
    dmask, q_decay, k_decay, chunk_decay = _retention_tables(c)
    row = lambda b, nn: b * n + nn
    whole = lambda b, nn: (0, 0, 0)
    return pl.pallas_call(
        _retention_kernel,
        grid=(batch, n),
        in_specs=[
            pl.BlockSpec((c, h * dk), lambda b, nn: (row(b, nn), 0)),
            pl.BlockSpec((c, h * dk), lambda b, nn: (row(b, nn), 1)),
            pl.BlockSpec((c, h * dv), lambda b, nn: (row(b, nn), 0)),
            pl.BlockSpec((c, h * dv), lambda b, nn: (row(b, nn), 0)),
            pl.BlockSpec((h, c, c), whole),
            pl.BlockSpec((h, c, 1), whole),
            pl.BlockSpec((h, c, 1), whole),
            pl.BlockSpec((h, 1, 1), whole),
        ],
        out_specs=pl.BlockSpec((c, h * dv), lambda b, nn: (row(b, nn), 0)),
        out_shape=jax.ShapeDtypeStruct((t, h * dv), BF16),
        scratch_shapes=[pltpu.VMEM((h, dk, dv), F32)],
        compiler_params=_cparams("parallel", "arbitrary"),
        name="retention",
    )(qk, qk, v, g, dmask, q_decay, k_decay, chunk_decay)


def _diff_attn_kernel(q_ref, k_ref, vt_ref, lam_ref, sg_ref, o_ref, qm_ref, m_ref, l_ref, acc_ref,
                      s0_ref, s1_ref, *, lambda_init):
    i = pl.program_id(2)
    tq = q_ref.shape[0]
    tk = vt_ref.shape[2]
    gw = min(ATTN_GROUP, tq)

    q = q_ref[...]
    lane = lax.broadcasted_iota(jnp.int32, q.shape, 1)
    zero = jnp.zeros_like(q)
    qm_ref[:tq, :] = jnp.where(lane < DIFF_HEAD_DIM, q, zero)
    qm_ref[tq:, :] = jnp.where(lane >= DIFF_HEAD_DIM, q, zero)
    m_ref[...] = jnp.full_like(m_ref, NEG_BIG)
    l_ref[...] = jnp.zeros_like(l_ref)
    acc_ref[...] = jnp.zeros_like(acc_ref)

    groups = [slice(g * gw, (g + 1) * gw) for g in range(2 * tq // gw)]

    def score_tile(s_ref, j):
        kt = k_ref[pl.ds(pl.multiple_of(j * tk, tk), tk), :]
        for cols in groups:
            s_ref[:, cols] = lax.dot_general(kt, qm_ref[cols, :], (((1,), (1,)), ((), ())),
                                             preferred_element_type=F32)

    def consume(s_ref, j, masked):
        vt = vt_ref[j]
        for g, cols in enumerate(groups):
            s = s_ref[:, cols]
            if masked:
                kpos = lax.broadcasted_iota(jnp.int32, s.shape, 0)
                qpos = (lax.broadcasted_iota(jnp.int32, s.shape, 1) + g * gw) & (tq - 1)
                s = jnp.where(kpos <= qpos, s, NEG_BIG)
            m_old = m_ref[:, cols]
            m_new = jnp.maximum(m_old, jnp.max(s, 0, keepdims=True))
            alpha = jnp.exp2(m_old - m_new)
            p = jnp.exp2(s - m_new)
            l_ref[:, cols] = alpha * l_ref[:, cols] + jnp.sum(p, 0, keepdims=True)
            acc_ref[:, cols] = alpha * acc_ref[:, cols] + jnp.dot(vt, p.astype(BF16),
                                                                  preferred_element_type=F32)
            m_ref[:, cols] = m_new

    score_tile(s0_ref, 0)

    def pair(jj, carry):
        j = 2 * jj
        score_tile(s1_ref, j + 1)
        consume(s0_ref, j, False)
        score_tile(s0_ref, j + 2)
        consume(s1_ref, j + 1, False)
        return carry

    lax.fori_loop(0, i // 2, pair, 0)

    @pl.when(i % 2 == 0)
    def _():
        consume(s0_ref, i, True)

    @pl.when(i % 2 == 1)
    def _():
        score_tile(s1_ref, i)
        consume(s0_ref, i - 1, False)
        consume(s1_ref, i, True)

    lam = lam_ref[...]
    lam_full = (jnp.exp(jnp.sum(lam[0:1] * lam[1:2], -1, keepdims=True))
                - jnp.exp(jnp.sum(lam[2:3] * lam[3:4], -1, keepdims=True)) + lambda_init)
    o1 = acc_ref[:, :tq] / l_ref[:, :tq]
    o2 = acc_ref[:, tq:] / l_ref[:, tq:]
    o = o1 - lam_full * o2
    o = o * lax.rsqrt(jnp.mean(o * o, 0, keepdims=True) + LN_EPS) * sg_ref[...] * (1.0 - lambda_init)
    o_ref[...] = o.T.astype(o_ref.dtype)


def _diff_attention(q, k, vt, lam, subln_g, lambda_init, batch, seq):
    t, d = q.shape
    tq = min(ATTN_TILE, seq)
    nq = seq // tq
    hd = 2 * DIFF_HEAD_DIM
    return pl.pallas_call(
        functools.partial(_diff_attn_kernel, lambda_init=lambda_init),
        grid=(batch, DIFF_HEADS, nq),
        in_specs=[
            pl.BlockSpec((tq, hd), lambda b, h, i: (b * nq + i, h)),
            pl.BlockSpec((seq, hd), lambda b, h, i: (b, h)),
            pl.BlockSpec((nq, hd, tq), lambda b, h, i: (b * DIFF_HEADS + h, 0, 0)),
            pl.BlockSpec((4, DIFF_HEAD_DIM), lambda b, h, i: (0, 0)),
            pl.BlockSpec((hd, 1), lambda b, h, i: (0, 0)),
        ],
        out_specs=pl.BlockSpec((tq, hd), lambda b, h, i: (b * nq + i, h)),
        out_shape=jax.ShapeDtypeStruct((t, d), BF16),
        scratch_shapes=[
            pltpu.VMEM((2 * tq, hd), BF16),
            pltpu.VMEM((1, 2 * tq), F32),
            pltpu.VMEM((1, 2 * tq), F32),
            pltpu.VMEM((hd, 2 * tq), F32),
            pltpu.VMEM((tq, 2 * tq), F32),
            pltpu.VMEM((tq, 2 * tq), F32),
        ],
        compiler_params=_cparams("parallel", "parallel", "arbitrary"),
        name="diff_attn",
    )(q, k, vt, lam, subln_g.reshape(hd, 1))


def _transpose_values(v, batch, seq):
    tk = min(ATTN_TILE, seq)
    nk = seq // tk
    hd = 2 * DIFF_HEAD_DIM
    v5 = v.reshape(batch, nk, tk, DIFF_HEADS, hd).transpose(0, 3, 1, 4, 2)
    return v5.reshape(batch * DIFF_HEADS * nk, hd, tk)


def _router_kernel(h_ref, w_ref, b_ref, idx_ref, gate_ref, cnt_ref):
    logits = jnp.dot(h_ref[...], w_ref[...], preferred_element_type=F32,
                     precision=lax.Precision.HIGHEST) + b_ref[...]
    lt = logits.T
    tm = lt.shape[1]
    row = lax.broadcasted_iota(jnp.int32, lt.shape, 0)
    cur = jnp.where(row < N_EXPERTS, lt, -jnp.inf)
    vals, idxs = [], []
    for _ in range(TOP_K):
        m = jnp.max(cur, 0, keepdims=True)
        sel = jnp.min(jnp.where(cur == m, row, ROUTER_PAD), 0, keepdims=True)
        vals.append(m)
        idxs.append(sel)
        cur = jnp.where(row == sel, -jnp.inf, cur)
    es = [jnp.exp(vk - vals[0]) for vk in vals]
    denom = es[0] + es[1] + es[2] + es[3]
    row8 = lax.broadcasted_iota(jnp.int32, (SUBLANES, tm), 0)
    idx_out = jnp.full((SUBLANES, tm), -1, jnp.int32)
    gate_out = jnp.zeros((SUBLANES, tm), F32)
    chosen = jnp.zeros(lt.shape, F32)
    for kk in range(TOP_K):
        idx_out = jnp.where(row8 == kk, idxs[kk], idx_out)
        gate_out = jnp.where(row8 == kk, es[kk] / denom, gate_out)
        chosen = jnp.where(row == idxs[kk], 1.0, chosen)
    idx_ref[...] = idx_out
    gate_ref[...] = gate_out
    ones = jnp.ones((SUBLANES, tm), BF16)
    cnt = lax.dot_general(ones, chosen.astype(BF16), (((1,), (1,)), ((), ())),
                          preferred_element_type=F32)
    cnt_ref[0] = cnt.astype(jnp.int32)


def _router(h, w_router, b_router):
    t, d = h.shape
    tm = min(MOE_TILE, t)
    nt = t // tm
    w = jnp.zeros((d, ROUTER_PAD), F32).at[:, :N_EXPERTS].set(w_router)
    b = jnp.zeros((1, ROUTER_PAD), F32).at[0, :N_EXPERTS].set(b_router)
    idx, gate, cnt = pl.pallas_call(
        _router_kernel,
        grid=(nt,),
        in_specs=[
            pl.BlockSpec((tm, d), lambda i: (i, 0)),
            pl.BlockSpec((d, ROUTER_PAD), lambda i: (0, 0)),
            pl.BlockSpec((1, ROUTER_PAD), lambda i: (0, 0)),
        ],
        out_specs=[
            pl.BlockSpec((SUBLANES, tm), lambda i: (0, i)),
            pl.BlockSpec((SUBLANES, tm), lambda i: (0, i)),
            pl.BlockSpec((1, SUBLANES, ROUTER_PAD), lambda i: (i, 0, 0)),
        ],
        out_shape=[
            jax.ShapeDtypeStruct((SUBLANES, t), jnp.int32),
            jax.ShapeDtypeStruct((SUBLANES, t), F32),
            jax.ShapeDtypeStruct((nt, SUBLANES, ROUTER_PAD), jnp.int32),
        ],
        compiler_params=_cparams("parallel"),
        name="router",
    )(h, w, b)
    return idx, gate, cnt[:, 0, :N_EXPERTS]


def _moe_plan(cnt, n_rows):
    a, blk = ROW_ALIGN, MOE_ROWS
    cnt_a = _round_up(cnt, a)
    rows_e = jnp.sum(cnt_a, 0)
    cap_e = _round_up(rows_e, blk)
    seg_end = jnp.cumsum(cap_e)
    seg_start = seg_end - cap_e
    gbase = seg_start[None, :] + jnp.cumsum(cnt_a, 0) - cnt_a
    loff = jnp.cumsum(cnt_a, 1) - cnt_a
    nch = cnt_a // a
    nb = n_rows // blk
    n_used = seg_end[-1] // blk
    block_row = jnp.minimum(jnp.arange(nb, dtype=jnp.int32), n_used - 1) * blk
    block_e = jnp.minimum(jnp.sum(seg_end[None, :] <= block_row[:, None], 1), N_EXPERTS - 1)
    loffv = jnp.zeros((cnt.shape[0], ROUTER_PAD, 1), F32).at[:, :N_EXPERTS, 0].set(loff.astype(F32))
    i32 = lambda x: x.astype(jnp.int32)
    return dict(nch=i32(nch), gbase=i32(gbase), loff=i32(loff), ntot=i32(jnp.sum(nch, 1)),
                pad_start=i32(seg_start + rows_e), pad_n=i32((cap_e - rows_e) // a),
                block_e=i32(block_e), n_used=i32(n_used).reshape(1), loffv=loffv)


def _local_positions(idx, loffv, u):
    tm = idx.shape[1]
    row = lax.broadcasted_iota(jnp.int32, (ROUTER_PAD, tm), 0)
    hits = [row == idx[kk:kk + 1, :] for kk in range(TOP_K)]
    chosen = jnp.zeros((ROUTER_PAD, tm), F32)
    for hit in hits:
        chosen = jnp.where(hit, 1.0, chosen)
    before = jnp.dot(chosen.astype(BF16), u, preferred_element_type=F32)
    pos = before + loffv
    return [jnp.sum(jnp.where(hit, pos, 0.0), 0, keepdims=True).astype(jnp.int32) for hit in hits]


def _selection_matrix(js, vals, n_local, tm):
    rr = lax.broadcasted_iota(jnp.int32, (n_local, tm), 0)
    out = jnp.zeros((n_local, tm), F32)
    for j, v in zip(js, vals):
        out = jnp.where(rr == j, v, out)
    return out


def _run_copies(nch_s, src_s, dst_s, i, copy):
    def per_expert(e, carry):
        src0 = src_s[i, e]
        dst0 = dst_s[i, e]

        def per_chunk(c, carry2):
            copy(pl.multiple_of(src0 + c * ROW_ALIGN, ROW_ALIGN),
                 pl.multiple_of(dst0 + c * ROW_ALIGN, ROW_ALIGN)).start()
            return carry2

        return lax.fori_loop(0, nch_s[i, e], per_chunk, carry)

    lax.fori_loop(0, N_EXPERTS, per_expert, 0)


def _drain(n, copy):
    def body(c, carry):
        copy(0, 0).wait()
        return carry

    lax.fori_loop(0, n, body, 0)


def _dispatch_kernel(nch_s, gbase_s, loff_s, ntot_s, pstart_s, pn_s, nu_s,
                     h_ref, idx_ref, loffv_ref, u_ref, xs_hbm, loc_ref, zero_ref, sem, zsem):
    i = pl.program_id(0)
    n_local, tm = loc_ref.shape[1], h_ref.shape[0]
    slot = i % 2
    js = _local_positions(idx_ref[...], loffv_ref[0], u_ref[...])
    perm = _selection_matrix(js, [1.0] * TOP_K, n_local, tm).astype(BF16)
    loc_ref[slot] = jnp.dot(perm, h_ref[...].astype(BF16), preferred_element_type=F32)

    def slot_copy(which):
        def copy(src_row, dst_row):
            return pltpu.make_async_copy(loc_ref.at[which, pl.ds(src_row, ROW_ALIGN)],
                                         xs_hbm.at[pl.ds(dst_row, ROW_ALIGN)], sem.at[which])
        return copy

    def zero_copy(src_row, dst_row):
        return pltpu.make_async_copy(zero_ref.at[pl.ds(src_row, ROW_ALIGN)],
                                     xs_hbm.at[pl.ds(dst_row, ROW_ALIGN)], zsem)

    def zero_block_copy(block):
        return pltpu.make_async_copy(
            zero_ref, xs_hbm.at[pl.ds(pl.multiple_of(block * MOE_ROWS, MOE_ROWS), MOE_ROWS)], zsem)

    _run_copies(nch_s, loff_s, gbase_s, i, slot_copy(slot))

    @pl.when(i > 0)
    def _():
        _drain(ntot_s[jnp.maximum(i - 1, 0)], slot_copy(1 - slot))

    @pl.when(i == pl.num_programs(0) - 1)
    def _():
        _drain(ntot_s[i], slot_copy(slot))
        zero_ref[...] = jnp.zeros_like(zero_ref)

        def per_expert(e, total):
            def per_chunk(c, carry):
                zero_copy(0, pl.multiple_of(pstart_s[e] + c * ROW_ALIGN, ROW_ALIGN)).start()
                return carry

            lax.fori_loop(0, pn_s[e], per_chunk, 0)
            return total + pn_s[e]

        total = lax.fori_loop(0, N_EXPERTS, per_expert, 0)
        _drain(total, zero_copy)
        n_blocks = xs_hbm.shape[0] // MOE_ROWS

        def start_block(blk, carry):
            zero_block_copy(blk).start()
            return carry

        def wait_block(blk, carry):
            zero_block_copy(blk).wait()
            return carry

        lax.fori_loop(nu_s[0], n_blocks, start_block, 0)
        lax.fori_loop(nu_s[0], n_blocks, wait_block, 0)


def _dispatch(h, idx, plan, u, n_rows):
    t, d = h.shape
    tm = min(MOE_TILE, t)
    n_local = _round_up(TOP_K * tm + N_EXPERTS * (ROW_ALIGN - 1), 2 * LANES)
    grid_spec = pltpu.PrefetchScalarGridSpec(
        num_scalar_prefetch=7,
        grid=(t // tm,),
        in_specs=[
            pl.BlockSpec((tm, d), lambda i, *_: (i, 0)),
            pl.BlockSpec((SUBLANES, tm), lambda i, *_: (0, i)),
            pl.BlockSpec((1, ROUTER_PAD, 1), lambda i, *_: (i, 0, 0)),
            pl.BlockSpec((tm, tm), lambda i, *_: (0, 0)),
        ],
        out_specs=pl.BlockSpec(memory_space=pl.ANY),
        scratch_shapes=[
            pltpu.VMEM((2, n_local, d), F32),
            pltpu.VMEM((MOE_ROWS, d), F32),
            pltpu.SemaphoreType.DMA((2,)),
            pltpu.SemaphoreType.DMA,
        ],
    )
    return pl.pallas_call(
        _dispatch_kernel,
        grid_spec=grid_spec,
        out_shape=jax.ShapeDtypeStruct((n_rows, d), F32),
        compiler_params=_cparams("arbitrary"),
        name="dispatch",
    )(plan["nch"], plan["gbase"], plan["loff"], plan["ntot"], plan["pad_start"], plan["pad_n"],
      plan["n_used"], h, idx, plan["loffv"], u)


def _expert_kernel(be_ref, nu_ref, x_ref, wu_ref, bu_ref, wd_ref, bd_ref, o_ref, wu_bf, wd_bf):
    i = pl.program_id(0)

    @pl.when(jnp.logical_or(i == 0, be_ref[i] != be_ref[jnp.maximum(i - 1, 0)]))
    def _():
        wu_bf[...] = wu_ref[0, 0].astype(BF16)
        wd_bf[...] = wd_ref[0, 0].astype(BF16)

    @pl.when(i < nu_ref[0])
    def _():
        hdn = jnp.dot(x_ref[...].astype(BF16), wu_bf[...], preferred_element_type=F32) + bu_ref[0, 0]
        glu = jnp.minimum(hdn[:, :D_FF], SWIGLU_LIMIT)
        lin = jnp.clip(hdn[:, D_FF:], -SWIGLU_LIMIT, SWIGLU_LIMIT)
        act = glu * jax.nn.sigmoid(SWIGLU_ALPHA * glu) * (lin + 1.0)
        o_ref[...] = jnp.dot(act.astype(BF16), wd_bf[...], preferred_element_type=F32) + bd_ref[0, 0]

    @pl.when(i >= nu_ref[0])
    def _():
        o_ref[...] = jnp.zeros_like(o_ref)


def _experts(xs, plan, w_up, b_up, w_down, b_down, layer):
    n_rows, d = xs.shape
    depth, e = w_up.shape[:2]
    used = lambda i, nu: jnp.maximum(jnp.minimum(i, nu[0] - 1), 0)
    grid_spec = pltpu.PrefetchScalarGridSpec(
        num_scalar_prefetch=2,
        grid=(n_rows // MOE_ROWS,),
        in_specs=[
            pl.BlockSpec((MOE_ROWS, d), lambda i, be, nu: (used(i, nu), 0)),
            pl.BlockSpec((1, 1, d, 2 * D_FF), lambda i, be, nu: (layer, be[i], 0, 0)),
            pl.BlockSpec((1, 1, 1, 2 * D_FF), lambda i, be, nu: (layer, be[i], 0, 0)),
            pl.BlockSpec((1, 1, D_FF, d), lambda i, be, nu: (layer, be[i], 0, 0)),
            pl.BlockSpec((1, 1, 1, d), lambda i, be, nu: (layer, be[i], 0, 0)),
        ],
        out_specs=pl.BlockSpec((MOE_ROWS, d), lambda i, be, nu: (i, 0)),
        scratch_shapes=[pltpu.VMEM((d, 2 * D_FF), BF16), pltpu.VMEM((D_FF, d), BF16)],
    )
    return pl.pallas_call(
        _expert_kernel,
        grid_spec=grid_spec,
        out_shape=jax.ShapeDtypeStruct((n_rows, d), F32),
        compiler_params=pltpu.CompilerParams(dimension_semantics=("arbitrary",),
                                             vmem_limit_bytes=EXPERT_VMEM_LIMIT),
        name="experts",
    )(plan["block_e"], plan["n_used"], xs, w_up, b_up.reshape(depth, e, 1, 2 * D_FF), w_down,
      b_down.reshape(depth, e, 1, d))


def _combine_kernel(nch_s, gbase_s, loff_s, ntot_s,
                    h_ref, idx_ref, gate_ref, loffv_ref, u_ref, g_ref, b_ref, ys_hbm,
                    o_ref, loc_ref, sem):
    i = pl.program_id(0)
    n_local, tm = loc_ref.shape[1], h_ref.shape[0]
    slot = i % 2

    def slot_copy(which):
        def copy(dst_row, src_row):
            return pltpu.make_async_copy(ys_hbm.at[pl.ds(src_row, ROW_ALIGN)],
                                         loc_ref.at[which, pl.ds(dst_row, ROW_ALIGN)],
                                         sem.at[which])
        return copy

    @pl.when(i == 0)
    def _():
        loc_ref[...] = jnp.zeros_like(loc_ref)
        _run_copies(nch_s, loff_s, gbase_s, 0, slot_copy(0))

    @pl.when(i + 1 < pl.num_programs(0))
    def _():
        _run_copies(nch_s, loff_s, gbase_s, i + 1, slot_copy(1 - slot))

    js = _local_positions(idx_ref[...], loffv_ref[0], u_ref[...])
    gates = gate_ref[...]
    row = lax.broadcasted_iota(jnp.int32, (LANES, tm), 0)
    stacked = jnp.zeros((LANES, tm), F32)
    for kk in range(TOP_K):
        stacked = jnp.where(row == kk, js[kk].astype(F32), stacked)
        stacked = jnp.where(row == TOP_K + kk, gates[kk:kk + 1, :], stacked)
    cols = stacked.T
    rr = lax.broadcasted_iota(jnp.int32, (tm, n_local), 1)
    gmat = jnp.zeros((tm, n_local), F32)
    for kk in range(TOP_K):
        gmat = jnp.where(rr == cols[:, kk:kk + 1].astype(jnp.int32),
                         cols[:, TOP_K + kk:TOP_K + kk + 1], gmat)
    _drain(ntot_s[i], slot_copy(slot))
    ffn = jnp.dot(gmat.astype(BF16), loc_ref[slot].astype(BF16), preferred_element_type=F32)
    o_ref[...] = _layer_norm(DN_ALPHA * h_ref[...] + ffn, g_ref[...], b_ref[...])


def _combine_ln(h, ys, idx, gate, plan, u, g, b):
    t, d = h.shape
    tm = min(MOE_TILE, t)
    n_local = _round_up(TOP_K * tm + N_EXPERTS * (ROW_ALIGN - 1), 2 * LANES)
    grid_spec = pltpu.PrefetchScalarGridSpec(
        num_scalar_prefetch=4,
        grid=(t // tm,),
        in_specs=[
            pl.BlockSpec((tm, d), lambda i, *_: (i, 0)),
            pl.BlockSpec((SUBLANES, tm), lambda i, *_: (0, i)),
            pl.BlockSpec((SUBLANES, tm), lambda i, *_: (0, i)),
            pl.BlockSpec((1, ROUTER_PAD, 1), lambda i, *_: (i, 0, 0)),
            pl.BlockSpec((tm, tm), lambda i, *_: (0, 0)),
            pl.BlockSpec((1, d), lambda i, *_: (0, 0)),
            pl.BlockSpec((1, d), lambda i, *_: (0, 0)),
            pl.BlockSpec(memory_space=pl.ANY),
        ],
        out_specs=pl.BlockSpec((tm, d), lambda i, *_: (i, 0)),
        scratch_shapes=[pltpu.VMEM((2, n_local, d), F32), pltpu.SemaphoreType.DMA((2,))],
    )
    return pl.pallas_call(
        _combine_kernel,
        grid_spec=grid_spec,
        out_shape=jax.ShapeDtypeStruct((t, d), F32),
        compiler_params=_cparams("arbitrary"),
        name="combine_ln",
    )(plan["nch"], plan["gbase"], plan["loff"], plan["ntot"],
      h, idx, gate, plan["loffv"], u, g.reshape(1, d), b.reshape(1, d), ys)


def _moe_ln(h, w_router, b_router, w_up, b_up, w_down, b_down, layer, ln_g, ln_b):
    t, d = h.shape
    tm = min(MOE_TILE, t)
    nt = t // tm
    n_rows = _round_up(t * TOP_K + nt * N_EXPERTS * (ROW_ALIGN - 1) + N_EXPERTS * (MOE_ROWS - 1),
                       MOE_ROWS)
    idx, gate, cnt = _router(h, w_router, b_router)
    plan = _moe_plan(cnt, n_rows)
    tri = jnp.arange(tm)
    u = (tri[:, None] < tri[None, :]).astype(BF16)
    xs = _dispatch(h, idx, plan, u, n_rows)
    ys = _experts(xs, plan, w_up, b_up, w_down, b_down, layer)
    return _combine_ln(h, ys, idx, gate, plan, u, ln_g, ln_b)


def _ret_rope_tables(seq):
    inv_freq = 1.0 / (RET_ROPE_THETA ** jnp.linspace(0.0, 1.0, RET_QK_DIM // 2, dtype=F32))
    ang = jnp.arange(seq, dtype=F32)[:, None] * inv_freq[None, :]
    cos, sin = jnp.cos(ang), jnp.sin(ang)
    return jnp.concatenate([cos, cos], -1), jnp.concatenate([-sin, sin], -1)


def _partial_rope_tables(seq):
    inv_freq = 1.0 / (ROPE_THETA ** (jnp.arange(0, ROPE_DIM, 2, dtype=F32) / ROPE_DIM))
    ang = jnp.arange(seq, dtype=F32)[:, None] * inv_freq[None, :]
    cos, sin = jnp.cos(ang), jnp.sin(ang)
    pad = jnp.zeros((seq, DIFF_HEAD_DIM - ROPE_DIM), F32)
    c64 = jnp.concatenate([cos, cos, pad + 1.0], -1)
    s64 = jnp.concatenate([-sin, sin, pad], -1)
    return jnp.concatenate([c64, c64], -1), jnp.concatenate([s64, s64], -1)


def kernel(x, ret_w_in, ret_w_out, kv_w, diff_w_q, diff_lambda, diff_subln_g, diff_w_out,
           ln_attn_g, ln_attn_b, ln_ffn_g, ln_ffn_b, moe_w_router, moe_b_router,
           moe_w_up, moe_b_up, moe_w_down, moe_b_down):
    batch, seq, d = x.shape
    t = batch * seq
    h = x.reshape(t, d)
    ret_cos, ret_sin = _ret_rope_tables(seq)
    par_cos, par_sin = _partial_rope_tables(seq)
    qk_scale = jnp.concatenate([jnp.ones((1, d), F32), jnp.full((1, d), RET_QK_DIM ** -0.5, F32)], -1)
    k_sh = v_sh = None
    for l in range(DEPTH):
        if l < N_A_LAYERS:
            w_in = ret_w_in[l].astype(BF16)
            qk = _proj(h, w_in[:, :2 * d], BF16, rope="half", cos=ret_cos, sin=ret_sin,
                       scale=qk_scale, seq=seq)
            v = _proj(h, w_in[:, 2 * d:2 * d + RET_V_WIDTH], BF16)
            g = _proj(h, w_in[:, 2 * d + RET_V_WIDTH:], F32)
            a = _retention(qk, v, g, batch, seq)
            h = _outproj_ln(a, ret_w_out[l].astype(BF16), h, ln_attn_g[l], ln_attn_b[l])
        else:
            j = l - N_A_LAYERS
            lambda_init = 0.8 - 0.6 * math.exp(-0.3 * l)
            q = _proj(h, diff_w_q[j].astype(BF16), BF16, rope="partial", cos=par_cos, sin=par_sin,
                      scale=jnp.full((1, d), DIFF_HEAD_DIM ** -0.5 * math.log2(math.e), F32), seq=seq)
            a = _diff_attention(q, k_sh, v_sh, diff_lambda[j], diff_subln_g[j], lambda_init, batch, seq)
            h = _outproj_ln(a, diff_w_out[j].astype(BF16), h, ln_attn_g[l], ln_attn_b[l])
        h = _moe_ln(h, moe_w_router[l], moe_b_router[l], moe_w_up, moe_b_up, moe_w_down, moe_b_down,
                    l, ln_ffn_g[l], ln_ffn_b[l])
        if l == N_A_LAYERS - 1:
            kvw = kv_w.astype(BF16)
            k_sh = _proj(h, kvw[:, :d], BF16, rope="partial", cos=par_cos, sin=par_sin, seq=seq)
            v_sh = _transpose_values(_proj(h, kvw[:, d:], BF16), batch, seq)
    return h.reshape(batch, seq, d)
```

```python
import functools
import math

import jax
import jax.numpy as jnp
from jax import lax
from jax.experimental import pallas as pl
from jax.experimental.pallas import tpu as pltpu

F32 = jnp.float32
BF16 = jnp.bfloat16

D_MODEL = 1024
DEPTH = 4
N_A_LAYERS = DEPTH // 2
RET_HEADS = 8
RET_QK_DIM = D_MODEL // RET_HEADS
RET_V_DIM = 2 * RET_QK_DIM
RET_V_WIDTH = RET_HEADS * RET_V_DIM
RET_ROPE_THETA = 10000.0
DIFF_HEAD_DIM = 64
DIFF_HEADS = D_MODEL // (2 * DIFF_HEAD_DIM)
DIFF_V_DIM = 2 * DIFF_HEAD_DIM
ROPE_THETA = 500000.0
ROPE_DIM = DIFF_HEAD_DIM // 4
NEG_BIG = -1e30
N_EXPERTS = 32
TOP_K = 4
D_FF = D_MODEL
SWIGLU_ALPHA = 1.702
SWIGLU_LIMIT = 7.0
DN_ALPHA = (2 * DEPTH) ** 0.25
LN_EPS = 1e-5

LANES = 128
SUBLANES = 8
VMEM_LIMIT = 48 * 1024 * 1024
EXPERT_VMEM_LIMIT = 58 * 1024 * 1024

ROW_TILE = 512
PROJ_TILE = 1024
RET_CHUNK = 256
ATTN_TILE = 512
ATTN_GROUP = 256
MOE_ROWS = 512
MOE_TILE = 512
RUN_BATCH = 4
DRAIN_BATCH = 8
ROW_ALIGN = SUBLANES
ROUTER_PAD = LANES


def _cparams(*sem):
    return pltpu.CompilerParams(dimension_semantics=sem, vmem_limit_bytes=VMEM_LIMIT)


def _round_up(x, m):
    return (x + m - 1) // m * m


def _layer_norm(y, g, b):
    mu = jnp.mean(y, -1, keepdims=True)
    d = y - mu
    var = jnp.mean(d * d, -1, keepdims=True)
    return d * lax.rsqrt(var + LN_EPS) * g + b


def _proj_kernel(x_ref, w_ref, c_ref, s_ref, scale_ref, o_ref, *, rope):
    acc = jnp.dot(x_ref[...].astype(BF16), w_ref[...], preferred_element_type=F32)
    tn = acc.shape[1]
    if rope is None:
        o_ref[...] = acc.astype(o_ref.dtype)
        return
    c = c_ref[...]
    s = s_ref[...]
    lane = lax.broadcasted_iota(jnp.int32, (1, LANES), 1)
    for g in range(tn // LANES):
        sl = slice(g * LANES, (g + 1) * LANES)
        xg = acc[:, sl]
        if rope == "half":
            y = xg * c + pltpu.roll(xg, LANES // 2, 1) * s
        else:
            half = ROPE_DIM // 2
            up = pltpu.roll(xg, LANES - half, 1)
            dn = pltpu.roll(xg, half, 1)
            partner = jnp.where((lane & half) == 0, up, dn)
            y = jnp.where((lane & (DIFF_HEAD_DIM - 1)) < ROPE_DIM, xg * c + partner * s, xg)
        o_ref[:, sl] = (y * scale_ref[:, sl]).astype(o_ref.dtype)


def _proj(x, w, out_dtype, *, rope=None, cos=None, sin=None, scale=None, seq=None, tn=1024):
    t, k = x.shape
    n = w.shape[1]
    tm = min(PROJ_TILE, t, seq or t)
    tn = min(tn, n)
    if cos is None:
        cos = jnp.zeros((tm, LANES), F32)
        sin = cos
        pos_blocks = 1
    else:
        pos_blocks = seq // tm
    if scale is None:
        scale = jnp.ones((1, n), F32)
    return pl.pallas_call(
        functools.partial(_proj_kernel, rope=rope),
        grid=(n // tn, t // tm),
        in_specs=[
            pl.BlockSpec((tm, k), lambda j, i: (i, 0)),
            pl.BlockSpec((k, tn), lambda j, i: (0, j)),
            pl.BlockSpec((tm, LANES), lambda j, i: (i % pos_blocks, 0)),
            pl.BlockSpec((tm, LANES), lambda j, i: (i % pos_blocks, 0)),
            pl.BlockSpec((1, tn), lambda j, i: (0, j)),
        ],
        out_specs=pl.BlockSpec((tm, tn), lambda j, i: (i, j)),
        out_shape=jax.ShapeDtypeStruct((t, n), out_dtype),
        compiler_params=_cparams("parallel", "parallel"),
        name="proj",
    )(x, w, cos, sin, scale)


def _outproj_ln_kernel(a_ref, w_ref, h_ref, g_ref, b_ref, o_ref):
    mix = jnp.dot(a_ref[...], w_ref[...], preferred_element_type=F32)
    o_ref[...] = _layer_norm(DN_ALPHA * h_ref[...] + mix, g_ref[...], b_ref[...])


def _outproj_ln(a, w, h, g, b):
    t, k = a.shape
    d = w.shape[1]
    tm = min(ROW_TILE, t)
    return pl.pallas_call(
        _outproj_ln_kernel,
        grid=(t // tm,),
        in_specs=[
            pl.BlockSpec((tm, k), lambda i: (i, 0)),
            pl.BlockSpec((k, d), lambda i: (0, 0)),
            pl.BlockSpec((tm, d), lambda i: (i, 0)),
            pl.BlockSpec((1, d), lambda i: (0, 0)),
            pl.BlockSpec((1, d), lambda i: (0, 0)),
        ],
        out_specs=pl.BlockSpec((tm, d), lambda i: (i, 0)),
        out_shape=jax.ShapeDtypeStruct((t, d), F32),
        compiler_params=_cparams("parallel"),
        name="outproj_ln",
    )(a, w, h, g.reshape(1, d), b.reshape(1, d))


def _retention_kernel(q_ref, k_ref, v_ref, g_ref, dm_ref, qd_ref, kd_ref, cd_ref, o_ref, state_ref):
    @pl.when(pl.program_id(1) == 0)
    def _():
        state_ref[...] = jnp.zeros_like(state_ref)

    dk, dv = RET_QK_DIM, RET_V_DIM
    heads = range(RET_HEADS)
    q = [q_ref[:, h * dk:(h + 1) * dk] for h in heads]
    k = [k_ref[:, h * dk:(h + 1) * dk] for h in heads]
    v = [v_ref[:, h * dv:(h + 1) * dv] for h in heads]
    s = [lax.dot_general(q[h], k[h], (((1,), (1,)), ((), ())), preferred_element_type=F32)
         for h in heads]
    state = [state_ref[h] for h in heads]
    cross = [jnp.dot(q[h], state[h].astype(BF16), preferred_element_type=F32) for h in heads]
    kdec = [(k[h].astype(F32) * kd_ref[h]).astype(BF16) for h in heads]
    kv = [lax.dot_general(kdec[h], v[h], (((0,), (0,)), ((), ())), preferred_element_type=F32)
          for h in heads]
    inner = [jnp.dot((s[h] * dm_ref[h]).astype(BF16), v[h], preferred_element_type=F32)
             for h in heads]
    for h in heads:
        state_ref[h] = cd_ref[h] * state[h] + kv[h]
        o = inner[h] + cross[h] * qd_ref[h]
        mu = jnp.mean(o, -1, keepdims=True)
        d = o - mu
        var = jnp.mean(d * d, -1, keepdims=True)
        gate = g_ref[:, h * dv:(h + 1) * dv]
        o_ref[:, h * dv:(h + 1) * dv] = (d * lax.rsqrt(var + LN_EPS)
                                         * (gate * jax.nn.sigmoid(gate))).astype(o_ref.dtype)


def _retention_tables(c):
    h = RET_HEADS
    log_gamma = jnp.log1p(-jnp.exp2(-5.0 - jnp.arange(h, dtype=F32)))
    idx = jnp.arange(c, dtype=F32)
    rel = idx[:, None] - idx[None, :]
    dmask = jnp.where(rel >= 0, jnp.exp(jnp.maximum(rel, 0.0) * log_gamma[:, None, None]), 0.0)
    k_decay = jnp.exp((c - 1 - idx) * log_gamma[:, None])[:, :, None]
    q_decay = jnp.exp((idx + 1) * log_gamma[:, None])[:, :, None]
    chunk_decay = jnp.exp(c * log_gamma)[:, None, None]
    return dmask, q_decay, k_decay, chunk_decay


def _retention(qk, v, g, batch, seq):
    t = qk.shape[0]
    c = min(RET_CHUNK, seq)
    n = seq // c
    h, dk, dv = RET_HEADS, RET_QK_DIM, RET_V_DIM
    dmask, q_decay, k_decay, chunk_decay = _retention_tables(c)
    row = lambda b, nn: b * n + nn
    whole = lambda b, nn: (0, 0, 0)
    return pl.pallas_call(
        _retention_kernel,
        grid=(batch, n),
        in_specs=[
            pl.BlockSpec((c, h * dk), lambda b, nn: (row(b, nn), 0)),
            pl.BlockSpec((c, h * dk), lambda b, nn: (row(b, nn), 1)),
            pl.BlockSpec((c, h * dv), lambda b, nn: (row(b, nn), 0)),
            pl.BlockSpec((c, h * dv), lambda b, nn: (row(b, nn), 0)),
            pl.BlockSpec((h, c, c), whole),
            pl.BlockSpec((h, c, 1), whole),
            pl.BlockSpec((h, c, 1), whole),
            pl.BlockSpec((h, 1, 1), whole),
        ],
        out_specs=pl.BlockSpec((c, h * dv), lambda b, nn: (row(b, nn), 0)),
        out_shape=jax.ShapeDtypeStruct((t, h * dv), BF16),
        scratch_shapes=[pltpu.VMEM((h, dk, dv), F32)],
        compiler_params=_cparams("parallel", "arbitrary"),
        name="retention",
    )(qk, qk, v, g, dmask, q_decay, k_decay, chunk_decay)


def _diff_attn_kernel(q_ref, k_ref, vt_ref, lam_ref, sg_ref, o_ref, qm_ref, m_ref, l_ref, acc_ref,
                      s0_ref, s1_ref, *, lambda_init):
    i = pl.program_id(2)
    tq = q_ref.shape[0]
    tk = vt_ref.shape[2]
    gw = min(ATTN_GROUP, tq)

    q = q_ref[...]
    lane = lax.broadcasted_iota(jnp.int32, q.shape, 1)
    zero = jnp.zeros_like(q)
    qm_ref[:tq, :] = jnp.where(lane < DIFF_HEAD_DIM, q, zero)
    qm_ref[tq:, :] = jnp.where(lane >= DIFF_HEAD_DIM, q, zero)
    m_ref[...] = jnp.full_like(m_ref, NEG_BIG)
    l_ref[...] = jnp.zeros_like(l_ref)
    acc_ref[...] = jnp.zeros_like(acc_ref)

    groups = [slice(g * gw, (g + 1) * gw) for g in range(2 * tq // gw)]

    def score_tile(s_ref, j):
        kt = k_ref[pl.ds(pl.multiple_of(j * tk, tk), tk), :]
        for cols in groups:
            s_ref[:, cols] = lax.dot_general(kt, qm_ref[cols, :], (((1,), (1,)), ((), ())),
                                             preferred_element_type=F32)

    def consume(s_ref, j, masked):
        vt = vt_ref[j]
        for g, cols in enumerate(groups):
            s = s_ref[:, cols]
            if masked:
                kpos = lax.broadcasted_iota(jnp.int32, s.shape, 0)
                qpos = (lax.broadcasted_iota(jnp.int32, s.shape, 1) + g * gw) & (tq - 1)
                s = jnp.where(kpos <= qpos, s, NEG_BIG)
            m_old = m_ref[:, cols]
            m_new = jnp.maximum(m_old, jnp.max(s, 0, keepdims=True))
            alpha = jnp.exp2(m_old - m_new)
            p = jnp.exp2(s - m_new)
            l_ref[:, cols] = alpha * l_ref[:, cols] + jnp.sum(p, 0, keepdims=True)
            acc_ref[:, cols] = alpha * acc_ref[:, cols] + jnp.dot(vt, p.astype(BF16),
                                                                  preferred_element_type=F32)
            m_ref[:, cols] = m_new

    score_tile(s0_ref, 0)

    def pair(jj, carry):
        j = 2 * jj
        score_tile(s1_ref, j + 1)
        consume(s0_ref, j, False)
        score_tile(s0_ref, j + 2)
        consume(s1_ref, j + 1, False)
        return carry

    lax.fori_loop(0, i // 2, pair, 0)

    @pl.when(i % 2 == 0)
    def _():
        consume(s0_ref, i, True)

    @pl.when(i % 2 == 1)
    def _():
        score_tile(s1_ref, i)
        consume(s0_ref, i - 1, False)
        consume(s1_ref, i, True)

    lam = lam_ref[...]
    lam_full = (jnp.exp(jnp.sum(lam[0:1] * lam[1:2], -1, keepdims=True))
                - jnp.exp(jnp.sum(lam[2:3] * lam[3:4], -1, keepdims=True)) + lambda_init)
    o1 = acc_ref[:, :tq] / l_ref[:, :tq]
    o2 = acc_ref[:, tq:] / l_ref[:, tq:]
    o = o1 - lam_full * o2
    o = o * lax.rsqrt(jnp.mean(o * o, 0, keepdims=True) + LN_EPS) * sg_ref[...] * (1.0 - lambda_init)
    o_ref[...] = o.T.astype(o_ref.dtype)


def _diff_attention(q, k, vt, lam, subln_g, lambda_init, batch, seq):
    t, d = q.shape
    tq = min(ATTN_TILE, seq)
    nq = seq // tq
    hd = 2 * DIFF_HEAD_DIM
    return pl.pallas_call(
        functools.partial(_diff_attn_kernel, lambda_init=lambda_init),
        grid=(batch, DIFF_HEADS, nq),
        in_specs=[
            pl.BlockSpec((tq, hd), lambda b, h, i: (b * nq + i, h)),
            pl.BlockSpec((seq, hd), lambda b, h, i: (b, h)),
            pl.BlockSpec((nq, hd, tq), lambda b, h, i: (b * DIFF_HEADS + h, 0, 0)),
            pl.BlockSpec((4, DIFF_HEAD_DIM), lambda b, h, i: (0, 0)),
            pl.BlockSpec((hd, 1), lambda b, h, i: (0, 0)),
        ],
        out_specs=pl.BlockSpec((tq, hd), lambda b, h, i: (b * nq + i, h)),
        out_shape=jax.ShapeDtypeStruct((t, d), BF16),
        scratch_shapes=[
            pltpu.VMEM((2 * tq, hd), BF16),
            pltpu.VMEM((1, 2 * tq), F32),
            pltpu.VMEM((1, 2 * tq), F32),
            pltpu.VMEM((hd, 2 * tq), F32),
            pltpu.VMEM((tq, 2 * tq), F32),
            pltpu.VMEM((tq, 2 * tq), F32),
        ],
        compiler_params=_cparams("parallel", "parallel", "arbitrary"),
        name="diff_attn",
    )(q, k, vt, lam, subln_g.reshape(hd, 1))


def _transpose_values(v, batch, seq):
    tk = min(ATTN_TILE, seq)
    nk = seq // tk
    hd = 2 * DIFF_HEAD_DIM
    v5 = v.reshape(batch, nk, tk, DIFF_HEADS, hd).transpose(0, 3, 1, 4, 2)
    return v5.reshape(batch * DIFF_HEADS * nk, hd, tk)


def _router_kernel(h_ref, whi_ref, wlo_ref, b_ref, idx_ref, gate_ref, cnt_ref):
    h = h_ref[...]
    h_hi = h.astype(BF16)
    h_lo = (h - h_hi.astype(F32)).astype(BF16)
    logits = (jnp.dot(h_hi, whi_ref[...], preferred_element_type=F32)
              + jnp.dot(h_lo, whi_ref[...], preferred_element_type=F32)
              + jnp.dot(h_hi, wlo_ref[...], preferred_element_type=F32)) + b_ref[...]
    lt = logits.T
    tm = lt.shape[1]
    row = lax.broadcasted_iota(jnp.int32, lt.shape, 0)
    cur = jnp.where(row < N_EXPERTS, lt, -jnp.inf)
    vals, idxs = [], []
    for _ in range(TOP_K):
        m = jnp.max(cur, 0, keepdims=True)
        sel = jnp.min(jnp.where(cur == m, row, ROUTER_PAD), 0, keepdims=True)
        vals.append(m)
        idxs.append(sel)
        cur = jnp.where(row == sel, -jnp.inf, cur)
    es = [jnp.exp(vk - vals[0]) for vk in vals]
    denom = es[0] + es[1] + es[2] + es[3]
    row8 = lax.broadcasted_iota(jnp.int32, (SUBLANES, tm), 0)
    idx_out = jnp.full((SUBLANES, tm), -1, jnp.int32)
    gate_out = jnp.zeros((SUBLANES, tm), F32)
    chosen = jnp.zeros(lt.shape, F32)
    for kk in range(TOP_K):
        idx_out = jnp.where(row8 == kk, idxs[kk], idx_out)
        gate_out = jnp.where(row8 == kk, es[kk] / denom, gate_out)
        chosen = jnp.where(row == idxs[kk], 1.0, chosen)
    idx_ref[...] = idx_out
    gate_ref[...] = gate_out
    ones = jnp.ones((SUBLANES, tm), BF16)
    cnt = lax.dot_general(ones, chosen.astype(BF16), (((1,), (1,)), ((), ())),
                          preferred_element_type=F32)
    cnt_ref[0] = cnt.astype(jnp.int32)


def _router(h, w_router, b_router):
    t, d = h.shape
    tm = min(MOE_TILE, t)
    nt = t // tm
    w = jnp.zeros((d, ROUTER_PAD), F32).at[:, :N_EXPERTS].set(w_router)
    b = jnp.zeros((1, ROUTER_PAD), F32).at[0, :N_EXPERTS].set(b_router)
    w_hi = w.astype(BF16)
    w_lo = (w - w_hi.astype(F32)).astype(BF16)
    idx, gate, cnt = pl.pallas_call(
        _router_kernel,
        grid=(nt,),
        in_specs=[
            pl.BlockSpec((tm, d), lambda i: (i, 0)),
            pl.BlockSpec((d, ROUTER_PAD), lambda i: (0, 0)),
            pl.BlockSpec((d, ROUTER_PAD), lambda i: (0, 0)),
            pl.BlockSpec((1, ROUTER_PAD), lambda i: (0, 0)),
        ],
        out_specs=[
            pl.BlockSpec((SUBLANES, tm), lambda i: (0, i)),
            pl.BlockSpec((SUBLANES, tm), lambda i: (0, i)),
            pl.BlockSpec((1, SUBLANES, ROUTER_PAD), lambda i: (i, 0, 0)),
        ],
        out_shape=[
            jax.ShapeDtypeStruct((SUBLANES, t), jnp.int32),
            jax.ShapeDtypeStruct((SUBLANES, t), F32),
            jax.ShapeDtypeStruct((nt, SUBLANES, ROUTER_PAD), jnp.int32),
        ],
        compiler_params=_cparams("parallel"),
        name="router",
    )(h, w_hi, w_lo, b)
    return idx, gate, cnt[:, 0, :N_EXPERTS]


def _moe_plan(cnt, n_rows):
    a, blk = ROW_ALIGN, MOE_ROWS
    cnt_a = _round_up(cnt, a)
    rows_e = jnp.sum(cnt_a, 0)
    cap_e = _round_up(rows_e, blk)
    seg_end = jnp.cumsum(cap_e)
    seg_start = seg_end - cap_e
    gbase = seg_start[None, :] + jnp.cumsum(cnt_a, 0) - cnt_a
    loff = jnp.cumsum(cnt_a, 1) - cnt_a
    nch = cnt_a // a
    nb = n_rows // blk
    n_used = seg_end[-1] // blk
    block_row = jnp.minimum(jnp.arange(nb, dtype=jnp.int32), n_used - 1) * blk
    block_e = jnp.minimum(jnp.sum(seg_end[None, :] <= block_row[:, None], 1), N_EXPERTS - 1)
    loffv = jnp.zeros((cnt.shape[0], ROUTER_PAD, 1), F32).at[:, :N_EXPERTS, 0].set(loff.astype(F32))
    i32 = lambda x: x.astype(jnp.int32)
    return dict(nch=i32(nch), gbase=i32(gbase), loff=i32(loff), ntot=i32(jnp.sum(nch, 1)),
                pad_start=i32(seg_start + rows_e), pad_n=i32((cap_e - rows_e) // a),
                block_e=i32(block_e), n_used=i32(n_used).reshape(1), loffv=loffv)


def _local_positions(idx, loffv, u):
    tm = idx.shape[1]
    row = lax.broadcasted_iota(jnp.int32, (ROUTER_PAD, tm), 0)
    hits = [row == idx[kk:kk + 1, :] for kk in range(TOP_K)]
    chosen = jnp.zeros((ROUTER_PAD, tm), F32)
    for hit in hits:
        chosen = jnp.where(hit, 1.0, chosen)
    before = jnp.dot(chosen.astype(BF16), u, preferred_element_type=F32)
    pos = before + loffv
    return [jnp.sum(jnp.where(hit, pos, 0.0), 0, keepdims=True).astype(jnp.int32) for hit in hits]


def _selection_matrix(js, vals, n_local, tm):
    rr = lax.broadcasted_iota(jnp.int32, (n_local, tm), 0)
    out = jnp.zeros((n_local, tm), F32)
    for j, v in zip(js, vals):
        out = jnp.where(rr == j, v, out)
    return out


def _run_copies(nch_s, src_s, dst_s, i, copy):
    big = RUN_BATCH * ROW_ALIGN

    def per_expert(e, carry):
        src0 = src_s[i, e]
        dst0 = dst_s[i, e]
        n = nch_s[i, e]
        n_big = lax.shift_right_logical(n, jnp.int32(RUN_BATCH.bit_length() - 1))

        def big_copy(c, carry2):
            copy(pl.multiple_of(src0 + c * big, ROW_ALIGN),
                 pl.multiple_of(dst0 + c * big, ROW_ALIGN), big).start()
            return carry2

        def small_copy(c, carry2):
            off = n_big * big + c * ROW_ALIGN
            copy(pl.multiple_of(src0 + off, ROW_ALIGN),
                 pl.multiple_of(dst0 + off, ROW_ALIGN)).start()
            return carry2

        lax.fori_loop(0, n_big, big_copy, 0)
        lax.fori_loop(0, n & (RUN_BATCH - 1), small_copy, 0)
        return carry

    lax.fori_loop(0, N_EXPERTS, per_expert, 0)


def _drain(n, copy):
    def many(c, carry):
        copy(0, 0, DRAIN_BATCH * ROW_ALIGN).wait()
        return carry

    def one(c, carry):
        copy(0, 0).wait()
        return carry

    lax.fori_loop(0, lax.shift_right_logical(n, jnp.int32(DRAIN_BATCH.bit_length() - 1)), many, 0)
    lax.fori_loop(0, n & (DRAIN_BATCH - 1), one, 0)


def _dispatch_kernel(nch_s, gbase_s, loff_s, ntot_s, pstart_s, pn_s, nu_s,
                     h_ref, idx_ref, loffv_ref, u_ref, xs_hbm, loc_ref, zero_ref, sem, zsem):
    i = pl.program_id(0)
    n_local, tm = loc_ref.shape[1], h_ref.shape[0]
    slot = i % 2
    js = _local_positions(idx_ref[...], loffv_ref[0], u_ref[...])
    perm = _selection_matrix(js, [1.0] * TOP_K, n_local, tm).astype(BF16)
    loc_ref[slot] = jnp.dot(perm, h_ref[...].astype(BF16), preferred_element_type=F32)

    def slot_copy(which):
        def copy(src_row, dst_row, rows=ROW_ALIGN):
            return pltpu.make_async_copy(loc_ref.at[which, pl.ds(src_row, rows)],
                                         xs_hbm.at[pl.ds(dst_row, rows)], sem.at[which])
        return copy

    def zero_copy(src_row, dst_row, rows=ROW_ALIGN):
        return pltpu.make_async_copy(zero_ref.at[pl.ds(src_row, rows)],
                                     xs_hbm.at[pl.ds(dst_row, rows)], zsem)

    def zero_block_copy(block):
        return pltpu.make_async_copy(
            zero_ref, xs_hbm.at[pl.ds(pl.multiple_of(block * MOE_ROWS, MOE_ROWS), MOE_ROWS)], zsem)

    _run_copies(nch_s, loff_s, gbase_s, i, slot_copy(slot))

    @pl.when(i > 0)
    def _():
        _drain(ntot_s[jnp.maximum(i - 1, 0)], slot_copy(1 - slot))

    @pl.when(i == pl.num_programs(0) - 1)
    def _():
        _drain(ntot_s[i], slot_copy(slot))
        zero_ref[...] = jnp.zeros_like(zero_ref)

        def per_expert(e, total):
            def per_chunk(c, carry):
                zero_copy(0, pl.multiple_of(pstart_s[e] + c * ROW_ALIGN, ROW_ALIGN)).start()
                return carry

            lax.fori_loop(0, pn_s[e], per_chunk, 0)
            return total + pn_s[e]

        total = lax.fori_loop(0, N_EXPERTS, per_expert, 0)
        _drain(total, zero_copy)
        n_blocks = xs_hbm.shape[0] // MOE_ROWS

        def start_block(blk, carry):
            zero_block_copy(blk).start()
            return carry

        def wait_block(blk, carry):
            zero_block_copy(blk).wait()
            return carry

        lax.fori_loop(nu_s[0], n_blocks, start_block, 0)
        lax.fori_loop(nu_s[0], n_blocks, wait_block, 0)


def _dispatch(h, idx, plan, u, n_rows):
    t, d = h.shape
    tm = min(MOE_TILE, t)
    n_local = _round_up(TOP_K * tm + N_EXPERTS * (ROW_ALIGN - 1), 2 * LANES)
    grid_spec = pltpu.PrefetchScalarGridSpec(
        num_scalar_prefetch=7,
        grid=(t // tm,),
        in_specs=[
            pl.BlockSpec((tm, d), lambda i, *_: (i, 0)),
            pl.BlockSpec((SUBLANES, tm), lambda i, *_: (0, i)),
            pl.BlockSpec((1, ROUTER_PAD, 1), lambda i, *_: (i, 0, 0)),
            pl.BlockSpec((tm, tm), lambda i, *_: (0, 0)),
        ],
        out_specs=pl.BlockSpec(memory_space=pl.ANY),
        scratch_shapes=[
            pltpu.VMEM((2, n_local, d), F32),
            pltpu.VMEM((MOE_ROWS, d), F32),
            pltpu.SemaphoreType.DMA((2,)),
            pltpu.SemaphoreType.DMA,
        ],
    )
    return pl.pallas_call(
        _dispatch_kernel,
        grid_spec=grid_spec,
        out_shape=jax.ShapeDtypeStruct((n_rows, d), F32),
        compiler_params=_cparams("arbitrary"),
        name="dispatch",
    )(plan["nch"], plan["gbase"], plan["loff"], plan["ntot"], plan["pad_start"], plan["pad_n"],
      plan["n_used"], h, idx, plan["loffv"], u)


def _expert_kernel(be_ref, nu_ref, x_ref, wu_ref, bu_ref, wd_ref, bd_ref, o_ref, wu_bf, wd_bf):
    i = pl.program_id(0)

    @pl.when(jnp.logical_or(i == 0, be_ref[i] != be_ref[jnp.maximum(i - 1, 0)]))
    def _():
        wu_bf[...] = wu_ref[0, 0].astype(BF16)
        wd_bf[...] = wd_ref[0, 0].astype(BF16)

    @pl.when(i < nu_ref[0])
    def _():
        hdn = jnp.dot(x_ref[...].astype(BF16), wu_bf[...], preferred_element_type=F32) + bu_ref[0, 0]
        glu = jnp.minimum(hdn[:, :D_FF], SWIGLU_LIMIT)
        lin = jnp.clip(hdn[:, D_FF:], -SWIGLU_LIMIT, SWIGLU_LIMIT)
        act = glu * jax.nn.sigmoid(SWIGLU_ALPHA * glu) * (lin + 1.0)
        o_ref[...] = jnp.dot(act.astype(BF16), wd_bf[...], preferred_element_type=F32) + bd_ref[0, 0]

    @pl.when(i >= nu_ref[0])
    def _():
        o_ref[...] = jnp.zeros_like(o_ref)


def _experts(xs, plan, w_up, b_up, w_down, b_down, layer):
    n_rows, d = xs.shape
    depth, e = w_up.shape[:2]
    used = lambda i, nu: jnp.maximum(jnp.minimum(i, nu[0] - 1), 0)
    grid_spec = pltpu.PrefetchScalarGridSpec(
        num_scalar_prefetch=2,
        grid=(n_rows // MOE_ROWS,),
        in_specs=[
            pl.BlockSpec((MOE_ROWS, d), lambda i, be, nu: (used(i, nu), 0)),
            pl.BlockSpec((1, 1, d, 2 * D_FF), lambda i, be, nu: (layer, be[i], 0, 0)),
            pl.BlockSpec((1, 1, 1, 2 * D_FF), lambda i, be, nu: (layer, be[i], 0, 0)),
            pl.BlockSpec((1, 1, D_FF, d), lambda i, be, nu: (layer, be[i], 0, 0)),
            pl.BlockSpec((1, 1, 1, d), lambda i, be, nu: (layer, be[i], 0, 0)),
        ],
        out_specs=pl.BlockSpec((MOE_ROWS, d), lambda i, be, nu: (i, 0)),
        scratch_shapes=[pltpu.VMEM((d, 2 * D_FF), BF16), pltpu.VMEM((D_FF, d), BF16)],
    )
    return pl.pallas_call(
        _expert_kernel,
        grid_spec=grid_spec,
        out_shape=jax.ShapeDtypeStruct((n_rows, d), F32),
        compiler_params=pltpu.CompilerParams(dimension_semantics=("arbitrary",),
                                             vmem_limit_bytes=EXPERT_VMEM_LIMIT),
        name="experts",
    )(plan["block_e"], plan["n_used"], xs, w_up, b_up.reshape(depth, e, 1, 2 * D_FF), w_down,
      b_down.reshape(depth, e, 1, d))


def _combine_kernel(nch_s, gbase_s, loff_s, ntot_s,
                    h_ref, idx_ref, gate_ref, loffv_ref, u_ref, g_ref, b_ref, ys_hbm,
                    o_ref, loc_ref, sem):
    i = pl.program_id(0)
    n_local, tm = loc_ref.shape[1], h_ref.shape[0]
    slot = i % 2

    def slot_copy(which):
        def copy(dst_row, src_row, rows=ROW_ALIGN):
            return pltpu.make_async_copy(ys_hbm.at[pl.ds(src_row, rows)],
                                         loc_ref.at[which, pl.ds(dst_row, rows)], sem.at[which])
        return copy

    @pl.when(i == 0)
    def _():
        loc_ref[...] = jnp.zeros_like(loc_ref)
        _run_copies(nch_s, loff_s, gbase_s, 0, slot_copy(0))

    @pl.when(i + 1 < pl.num_programs(0))
    def _():
        _run_copies(nch_s, loff_s, gbase_s, i + 1, slot_copy(1 - slot))

    js = _local_positions(idx_ref[...], loffv_ref[0], u_ref[...])
    gates = gate_ref[...]
    row = lax.broadcasted_iota(jnp.int32, (LANES, tm), 0)
    stacked = jnp.zeros((LANES, tm), F32)
    for kk in range(TOP_K):
        stacked = jnp.where(row == kk, js[kk].astype(F32), stacked)
        stacked = jnp.where(row == TOP_K + kk, gates[kk:kk + 1, :], stacked)
    cols = stacked.T
    rr = lax.broadcasted_iota(jnp.int32, (tm, n_local), 1)
    gmat = jnp.zeros((tm, n_local), F32)
    for kk in range(TOP_K):
        gmat = jnp.where(rr == cols[:, kk:kk + 1].astype(jnp.int32),
                         cols[:, TOP_K + kk:TOP_K + kk + 1], gmat)
    _drain(ntot_s[i], slot_copy(slot))
    ffn = jnp.dot(gmat.astype(BF16), loc_ref[slot].astype(BF16), preferred_element_type=F32)
    o_ref[...] = _layer_norm(DN_ALPHA * h_ref[...] + ffn, g_ref[...], b_ref[...])


def _combine_ln(h, ys, idx, gate, plan, u, g, b):
    t, d = h.shape
    tm = min(MOE_TILE, t)
    n_local = _round_up(TOP_K * tm + N_EXPERTS * (ROW_ALIGN - 1), 2 * LANES)
    grid_spec = pltpu.PrefetchScalarGridSpec(
        num_scalar_prefetch=4,
        grid=(t // tm,),
        in_specs=[
            pl.BlockSpec((tm, d), lambda i, *_: (i, 0)),
            pl.BlockSpec((SUBLANES, tm), lambda i, *_: (0, i)),
            pl.BlockSpec((SUBLANES, tm), lambda i, *_: (0, i)),
            pl.BlockSpec((1, ROUTER_PAD, 1), lambda i, *_: (i, 0, 0)),
            pl.BlockSpec((tm, tm), lambda i, *_: (0, 0)),
            pl.BlockSpec((1, d), lambda i, *_: (0, 0)),
            pl.BlockSpec((1, d), lambda i, *_: (0, 0)),
            pl.BlockSpec(memory_space=pl.ANY),
        ],
        out_specs=pl.BlockSpec((tm, d), lambda i, *_: (i, 0)),
        scratch_shapes=[pltpu.VMEM((2, n_local, d), F32), pltpu.SemaphoreType.DMA((2,))],
    )
    return pl.pallas_call(
        _combine_kernel,
        grid_spec=grid_spec,
        out_shape=jax.ShapeDtypeStruct((t, d), F32),
        compiler_params=_cparams("arbitrary"),
        name="combine_ln",
    )(plan["nch"], plan["gbase"], plan["loff"], plan["ntot"],
      h, idx, gate, plan["loffv"], u, g.reshape(1, d), b.reshape(1, d), ys)


def _moe_ln(h, w_router, b_router, w_up, b_up, w_down, b_down, layer, ln_g, ln_b):
    t, d = h.shape
    tm = min(MOE_TILE, t)
    nt = t // tm
    n_rows = _round_up(t * TOP_K + nt * N_EXPERTS * (ROW_ALIGN - 1) + N_EXPERTS * (MOE_ROWS - 1),
                       MOE_ROWS)
    idx, gate, cnt = _router(h, w_router, b_router)
    plan = _moe_plan(cnt, n_rows)
    tri = jnp.arange(tm)
    u = (tri[:, None] < tri[None, :]).astype(BF16)
    xs = _dispatch(h, idx, plan, u, n_rows)
    ys = _experts(xs, plan, w_up, b_up, w_down, b_down, layer)
    return _combine_ln(h, ys, idx, gate, plan, u, ln_g, ln_b)


def _ret_rope_tables(seq):
    inv_freq = 1.0 / (RET_ROPE_THETA ** jnp.linspace(0.0, 1.0, RET_QK_DIM // 2, dtype=F32))
    ang = jnp.arange(seq, dtype=F32)[:, None] * inv_freq[None, :]
    cos, sin = jnp.cos(ang), jnp.sin(ang)
    return jnp.concatenate([cos, cos], -1), jnp.concatenate([-sin, sin], -1)


def _partial_rope_tables(seq):
    inv_freq = 1.0 / (ROPE_THETA ** (jnp.arange(0, ROPE_DIM, 2, dtype=F32) / ROPE_DIM))
    ang = jnp.arange(seq, dtype=F32)[:, None] * inv_freq[None, :]
    cos, sin = jnp.cos(ang), jnp.sin(ang)
    pad = jnp.zeros((seq, DIFF_HEAD_DIM - ROPE_DIM), F32)
    c64 = jnp.concatenate([cos, cos, pad + 1.0], -1)
    s64 = jnp.concatenate([-sin, sin, pad], -1)
    return jnp.concatenate([c64, c64], -1), jnp.concatenate([s64, s64], -1)


def kernel(x, ret_w_in, ret_w_out, kv_w, diff_w_q, diff_lambda, diff_subln_g, diff_w_out,
           ln_attn_g, ln_attn_b, ln_ffn_g, ln_ffn_b, moe_w_router, moe_b_router,
           moe_w_up, moe_b_up, moe_w_down, moe_b_down):
    batch, seq, d = x.shape
    t = batch * seq
    h = x.reshape(t, d)
    ret_cos, ret_sin = _ret_rope_tables(seq)
    par_cos, par_sin = _partial_rope_tables(seq)
    qk_scale = jnp.concatenate([jnp.ones((1, d), F32), jnp.full((1, d), RET_QK_DIM ** -0.5, F32)], -1)
    k_sh = v_sh = None
    for l in range(DEPTH):
        if l < N_A_LAYERS:
            w_in = ret_w_in[l].astype(BF16)
            qk = _proj(h, w_in[:, :2 * d], BF16, rope="half", cos=ret_cos, sin=ret_sin,
                       scale=qk_scale, seq=seq)
            v = _proj(h, w_in[:, 2 * d:2 * d + RET_V_WIDTH], BF16)
            g = _proj(h, w_in[:, 2 * d + RET_V_WIDTH:], F32)
            a = _retention(qk, v, g, batch, seq)
            h = _outproj_ln(a, ret_w_out[l].astype(BF16), h, ln_attn_g[l], ln_attn_b[l])
        else:
            j = l - N_A_LAYERS
            lambda_init = 0.8 - 0.6 * math.exp(-0.3 * l)
            q = _proj(h, diff_w_q[j].astype(BF16), BF16, rope="partial", cos=par_cos, sin=par_sin,
                      scale=jnp.full((1, d), DIFF_HEAD_DIM ** -0.5 * math.log2(math.e), F32), seq=seq)
            a = _diff_attention(q, k_sh, v_sh, diff_lambda[j], diff_subln_g[j], lambda_init, batch, seq)
            h = _outproj_ln(a, diff_w_out[j].astype(BF16), h, ln_attn_g[l], ln_attn_b[l])
        h = _moe_ln(h, moe_w_router[l], moe_b_router[l], moe_w_up, moe_b_up, moe_w_down, moe_b_down,
                    l, ln_ffn_g[l], ln_ffn_b[l])
        if l == N_A_LAYERS - 1:
            kvw = kv_w.astype(BF16)
            k_sh = _proj(h, kvw[:, :d], BF16, rope="partial", cos=par_cos, sin=par_sin, seq=seq)
            v_sh = _transpose_values(_proj(h, kvw[:, d:], BF16), batch, seq)
    return h.reshape(batch, seq, d)
```

```python
import functools
import math

import jax
import jax.numpy as jnp
from jax import lax
from jax.experimental import pallas as pl
from jax.experimental.pallas import tpu as pltpu

F32 = jnp.float32
BF16 = jnp.bfloat16

D_MODEL = 1024
DEPTH = 4
N_A_LAYERS = DEPTH // 2
RET_HEADS = 8
RET_QK_DIM = D_MODEL // RET_HEADS
RET_V_DIM = 2 * RET_QK_DIM
RET_V_WIDTH = RET_HEADS * RET_V_DIM
RET_ROPE_THETA = 10000.0
DIFF_HEAD_DIM = 64
DIFF_HEADS = D_MODEL // (2 * DIFF_HEAD_DIM)
DIFF_V_DIM = 2 * DIFF_HEAD_DIM
ROPE_THETA = 500000.0
ROPE_DIM = DIFF_HEAD_DIM // 4
NEG_BIG = -1e30
N_EXPERTS = 32
TOP_K = 4
D_FF = D_MODEL
SWIGLU_ALPHA = 1.702
SWIGLU_LIMIT = 7.0
DN_ALPHA = (2 * DEPTH) ** 0.25
LN_EPS = 1e-5

LANES = 128
SUBLANES = 8
VMEM_LIMIT = 48 * 1024 * 1024
EXPERT_VMEM_LIMIT = 58 * 1024 * 1024

ROW_TILE = 512
PROJ_TILE = 1024
RET_CHUNK = 256
ATTN_TILE = 512
ATTN_GROUP = 256
MOE_ROWS = 512
MOE_TILE = 512
RUN_BATCH = 4
DRAIN_BATCH = 8
ROW_ALIGN = SUBLANES
ROUTER_PAD = LANES


def _cparams(*sem):
    return pltpu.CompilerParams(dimension_semantics=sem, vmem_limit_bytes=VMEM_LIMIT)


def _round_up(x, m):
    return (x + m - 1) // m * m


def _layer_norm(y, g, b):
    mu = jnp.mean(y, -1, keepdims=True)
    d = y - mu
    var = jnp.mean(d * d, -1, keepdims=True)
    return d * lax.rsqrt(var + LN_EPS) * g + b


def _proj_kernel(x_ref, w_ref, c_ref, s_ref, scale_ref, o_ref, *, rope):
    acc = jnp.dot(x_ref[...].astype(BF16), w_ref[...], preferred_element_type=F32)
    tn = acc.shape[1]
    if rope is None:
        o_ref[...] = acc.astype(o_ref.dtype)
        return
    c = c_ref[...]
    s = s_ref[...]
    lane = lax.broadcasted_iota(jnp.int32, (1, LANES), 1)
    for g in range(tn // LANES):
        sl = slice(g * LANES, (g + 1) * LANES)
        xg = acc[:, sl]
        if rope == "half":
            y = xg * c + pltpu.roll(xg, LANES // 2, 1) * s
        else:
            half = ROPE_DIM // 2
            up = pltpu.roll(xg, LANES - half, 1)
            dn = pltpu.roll(xg, half, 1)
            partner = jnp.where((lane & half) == 0, up, dn)
            y = jnp.where((lane & (DIFF_HEAD_DIM - 1)) < ROPE_DIM, xg * c + partner * s, xg)
        o_ref[:, sl] = (y * scale_ref[:, sl]).astype(o_ref.dtype)


def _proj(x, w, out_dtype, *, rope=None, cos=None, sin=None, scale=None, seq=None, tn=1024):
    t, k = x.shape
    n = w.shape[1]
    tm = min(PROJ_TILE, t, seq or t)
    tn = min(tn, n)
    if cos is None:
        cos = jnp.zeros((tm, LANES), F32)
        sin = cos
        pos_blocks = 1
    else:
        pos_blocks = seq // tm
    if scale is None:
        scale = jnp.ones((1, n), F32)
    return pl.pallas_call(
        functools.partial(_proj_kernel, rope=rope),
        grid=(n // tn, t // tm),
        in_specs=[
            pl.BlockSpec((tm, k), lambda j, i: (i, 0)),
            pl.BlockSpec((k, tn), lambda j, i: (0, j)),
            pl.BlockSpec((tm, LANES), lambda j, i: (i % pos_blocks, 0)),
            pl.BlockSpec((tm, LANES), lambda j, i: (i % pos_blocks, 0)),
            pl.BlockSpec((1, tn), lambda j, i: (0, j)),
        ],
        out_specs=pl.BlockSpec((tm, tn), lambda j, i: (i, j)),
        out_shape=jax.ShapeDtypeStruct((t, n), out_dtype),
        compiler_params=_cparams("parallel", "parallel"),
        name="proj",
    )(x, w, cos, sin, scale)


def _outproj_ln_kernel(a_ref, w_ref, h_ref, g_ref, b_ref, o_ref):
    mix = jnp.dot(a_ref[...], w_ref[...], preferred_element_type=F32)
    o_ref[...] = _layer_norm(DN_ALPHA * h_ref[...] + mix, g_ref[...], b_ref[...])


def _outproj_ln(a, w, h, g, b):
    t, k = a.shape
    d = w.shape[1]
    tm = min(ROW_TILE, t)
    return pl.pallas_call(
        _outproj_ln_kernel,
        grid=(t // tm,),
        in_specs=[
            pl.BlockSpec((tm, k), lambda i: (i, 0)),
            pl.BlockSpec((k, d), lambda i: (0, 0)),
            pl.BlockSpec((tm, d), lambda i: (i, 0)),
            pl.BlockSpec((1, d), lambda i: (0, 0)),
            pl.BlockSpec((1, d), lambda i: (0, 0)),
        ],
        out_specs=pl.BlockSpec((tm, d), lambda i: (i, 0)),
        out_shape=jax.ShapeDtypeStruct((t, d), F32),
        compiler_params=_cparams("parallel"),
        name="outproj_ln",
    )(a, w, h, g.reshape(1, d), b.reshape(1, d))


def _retention_kernel(q_ref, k_ref, v_ref, g_ref, dm_ref, qd_ref, kd_ref, cd_ref, o_ref, state_ref):
    @pl.when(pl.program_id(1) == 0)
    def _():
        state_ref[...] = jnp.zeros_like(state_ref)

    dk, dv = RET_QK_DIM, RET_V_DIM
    heads = range(RET_HEADS)
    q = [q_ref[:, h * dk:(h + 1) * dk] for h in heads]
    k = [k_ref[:, h * dk:(h + 1) * dk] for h in heads]
    v = [v_ref[:, h * dv:(h + 1) * dv] for h in heads]
    s = [lax.dot_general(q[h], k[h], (((1,), (1,)), ((), ())), preferred_element_type=F32)
         for h in heads]
    state = [state_ref[h] for h in heads]
    cross = [jnp.dot(q[h], state[h].astype(BF16), preferred_element_type=F32) for h in heads]
    kdec = [(k[h].astype(F32) * kd_ref[h]).astype(BF16) for h in heads]
    kv = [lax.dot_general(kdec[h], v[h], (((0,), (0,)), ((), ())), preferred_element_type=F32)
          for h in heads]
    inner = [jnp.dot((s[h] * dm_ref[h]).astype(BF16), v[h], preferred_element_type=F32)
             for h in heads]
    for h in heads:
        state_ref[h] = cd_ref[h] * state[h] + kv[h]
        o = inner[h] + cross[h] * qd_ref[h]
        mu = jnp.mean(o, -1, keepdims=True)
        d = o - mu
        var = jnp.mean(d * d, -1, keepdims=True)
        gate = g_ref[:, h * dv:(h + 1) * dv]
        o_ref[:, h * dv:(h + 1) * dv] = (d * lax.rsqrt(var + LN_EPS)
                                         * (gate * jax.nn.sigmoid(gate))).astype(o_ref.dtype)


def _retention_tables(c):
    h = RET_HEADS
    log_gamma = jnp.log1p(-jnp.exp2(-5.0 - jnp.arange(h, dtype=F32)))
    idx = jnp.arange(c, dtype=F32)
    rel = idx[:, None] - idx[None, :]
    dmask = jnp.where(rel >= 0, jnp.exp(jnp.maximum(rel, 0.0) * log_gamma[:, None, None]), 0.0)
    k_decay = jnp.exp((c - 1 - idx) * log_gamma[:, None])[:, :, None]
    q_decay = jnp.exp((idx + 1) * log_gamma[:, None])[:, :, None]
    chunk_decay = jnp.exp(c * log_gamma)[:, None, None]
    return dmask, q_decay, k_decay, chunk_decay


def _retention(qk, v, g, batch, seq):
    t = qk.shape[0]
    c = min(RET_CHUNK, seq)
    n = seq // c
    h, dk, dv = RET_HEADS, RET_QK_DIM, RET_V_DIM
    dmask, q_decay, k_decay, chunk_decay = _retention_tables(c)
    row = lambda b, nn: b * n + nn
    whole = lambda b, nn: (0, 0, 0)
    return pl.pallas_call(
        _retention_kernel,
        grid=(batch, n),
        in_specs=[
            pl.BlockSpec((c, h * dk), lambda b, nn: (row(b, nn), 0)),
            pl.BlockSpec((c, h * dk), lambda b, nn: (row(b, nn), 1)),
            pl.BlockSpec((c, h * dv), lambda b, nn: (row(b, nn), 0)),
            pl.BlockSpec((c, h * dv), lambda b, nn: (row(b, nn), 0)),
            pl.BlockSpec((h, c, c), whole),
            pl.BlockSpec((h, c, 1), whole),
            pl.BlockSpec((h, c, 1), whole),
            pl.BlockSpec((h, 1, 1), whole),
        ],
        out_specs=pl.BlockSpec((c, h * dv), lambda b, nn: (row(b, nn), 0)),
        out_shape=jax.ShapeDtypeStruct((t, h * dv), BF16),
        scratch_shapes=[pltpu.VMEM((h, dk, dv), F32)],
        compiler_params=_cparams("parallel", "arbitrary"),
        name="retention",
    )(qk, qk, v, g, dmask, q_decay, k_decay, chunk_decay)


def _diff_attn_kernel(q_ref, k_ref, vt_ref, lam_ref, sg_ref, o_ref, qm_ref, m_ref, l_ref, acc_ref,
                      s0_ref, s1_ref, *, lambda_init):
    tk = vt_ref.shape[2]
    tq = tk
    gw = min(ATTN_GROUP, tq)
    groups = [slice(g * gw, (g + 1) * gw) for g in range(2 * tq // gw)]
    lam = lam_ref[...]
    lam_full = (jnp.exp(jnp.sum(lam[0:1] * lam[1:2], -1, keepdims=True))
                - jnp.exp(jnp.sum(lam[2:3] * lam[3:4], -1, keepdims=True)) + lambda_init)

    def score_tile(s_ref, j):
        kt = k_ref[pl.ds(pl.multiple_of(j * tk, tk), tk), :]
        for cols in groups:
            s_ref[:, cols] = lax.dot_general(kt, qm_ref[cols, :], (((1,), (1,)), ((), ())),
                                             preferred_element_type=F32)

    def consume(s_ref, j, masked):
        vt = vt_ref[j]
        for g, cols in enumerate(groups):
            s = s_ref[:, cols]
            if masked:
                kpos = lax.broadcasted_iota(jnp.int32, s.shape, 0)
                qpos = (lax.broadcasted_iota(jnp.int32, s.shape, 1) + g * gw) & (tq - 1)
                s = jnp.where(kpos <= qpos, s, NEG_BIG)
            m_old = m_ref[:, cols]
            m_new = jnp.maximum(m_old, jnp.max(s, 0, keepdims=True))
            alpha = jnp.exp2(m_old - m_new)
            p = jnp.exp2(s - m_new)
            l_ref[:, cols] = alpha * l_ref[:, cols] + jnp.sum(p, 0, keepdims=True)
            acc_ref[:, cols] = alpha * acc_ref[:, cols] + jnp.dot(vt, p.astype(BF16),
                                                                  preferred_element_type=F32)
            m_ref[:, cols] = m_new

    def query_tile(i, carry):
        rows = pl.ds(pl.multiple_of(i * tq, tq), tq)
        q = q_ref[rows, :]
        lane = lax.broadcasted_iota(jnp.int32, q.shape, 1)
        zero = jnp.zeros_like(q)
        qm_ref[:tq, :] = jnp.where(lane < DIFF_HEAD_DIM, q, zero)
        qm_ref[tq:, :] = jnp.where(lane >= DIFF_HEAD_DIM, q, zero)
        m_ref[...] = jnp.full_like(m_ref, NEG_BIG)
        l_ref[...] = jnp.zeros_like(l_ref)
        acc_ref[...] = jnp.zeros_like(acc_ref)

        score_tile(s0_ref, 0)

        def pair(jj, carry2):
            j = 2 * jj
            score_tile(s1_ref, j + 1)
            consume(s0_ref, j, False)
            score_tile(s0_ref, j + 2)
            consume(s1_ref, j + 1, False)
            return carry2

        lax.fori_loop(0, i // 2, pair, 0)

        @pl.when(i % 2 == 0)
        def _():
            consume(s0_ref, i, True)

        @pl.when(i % 2 == 1)
        def _():
            score_tile(s1_ref, i)
            consume(s0_ref, i - 1, False)
            consume(s1_ref, i, True)

        o1 = acc_ref[:, :tq] / l_ref[:, :tq]
        o2 = acc_ref[:, tq:] / l_ref[:, tq:]
        o = o1 - lam_full * o2
        o = (o * lax.rsqrt(jnp.mean(o * o, 0, keepdims=True) + LN_EPS) * sg_ref[...]
             * (1.0 - lambda_init))
        o_ref[rows, :] = o.T.astype(o_ref.dtype)
        return carry

    lax.fori_loop(0, q_ref.shape[0] // tq, query_tile, 0)


def _diff_attention(q, k, vt, lam, subln_g, lambda_init, batch, seq):
    t, d = q.shape
    tq = min(ATTN_TILE, seq)
    nq = seq // tq
    hd = 2 * DIFF_HEAD_DIM
    return pl.pallas_call(
        functools.partial(_diff_attn_kernel, lambda_init=lambda_init),
        grid=(batch, DIFF_HEADS),
        in_specs=[
            pl.BlockSpec((seq, hd), lambda b, h: (b, h)),
            pl.BlockSpec((seq, hd), lambda b, h: (b, h)),
            pl.BlockSpec((nq, hd, tq), lambda b, h: (b * DIFF_HEADS + h, 0, 0)),
            pl.BlockSpec((4, DIFF_HEAD_DIM), lambda b, h: (0, 0)),
            pl.BlockSpec((hd, 1), lambda b, h: (0, 0)),
        ],
        out_specs=pl.BlockSpec((seq, hd), lambda b, h: (b, h)),
        out_shape=jax.ShapeDtypeStruct((t, d), BF16),
        scratch_shapes=[
            pltpu.VMEM((2 * tq, hd), BF16),
            pltpu.VMEM((1, 2 * tq), F32),
            pltpu.VMEM((1, 2 * tq), F32),
            pltpu.VMEM((hd, 2 * tq), F32),
            pltpu.VMEM((tq, 2 * tq), F32),
            pltpu.VMEM((tq, 2 * tq), F32),
        ],
        compiler_params=_cparams("parallel", "parallel"),
        name="diff_attn",
    )(q, k, vt, lam, subln_g.reshape(hd, 1))


def _transpose_values(v, batch, seq):
    tk = min(ATTN_TILE, seq)
    nk = seq // tk
    hd = 2 * DIFF_HEAD_DIM
    v5 = v.reshape(batch, nk, tk, DIFF_HEADS, hd).transpose(0, 3, 1, 4, 2)
    return v5.reshape(batch * DIFF_HEADS * nk, hd, tk)


def _split_bf16(x):
    hi = x.astype(BF16)
    return hi, (x - hi.astype(F32)).astype(BF16)


def _router_kernel(h_ref, w_ref, b_ref, idx_ref, gate_ref, cnt_ref):
    h_hi, h_lo = _split_bf16(h_ref[...])
    w_hi, w_lo = _split_bf16(w_ref[...])
    logits = (jnp.dot(h_hi, w_hi, preferred_element_type=F32)
              + jnp.dot(h_lo, w_hi, preferred_element_type=F32)
              + jnp.dot(h_hi, w_lo, preferred_element_type=F32)) + b_ref[...]
    lt = logits.T
    tm = lt.shape[1]
    row = lax.broadcasted_iota(jnp.int32, lt.shape, 0)
    cur = jnp.where(row < N_EXPERTS, lt, -jnp.inf)
    vals, idxs = [], []
    for _ in range(TOP_K):
        m = jnp.max(cur, 0, keepdims=True)
        sel = jnp.min(jnp.where(cur == m, row, ROUTER_PAD), 0, keepdims=True)
        vals.append(m)
        idxs.append(sel)
        cur = jnp.where(row == sel, -jnp.inf, cur)
    es = [jnp.exp(vk - vals[0]) for vk in vals]
    denom = es[0] + es[1] + es[2] + es[3]
    row8 = lax.broadcasted_iota(jnp.int32, (SUBLANES, tm), 0)
    idx_out = jnp.full((SUBLANES, tm), -1, jnp.int32)
    gate_out = jnp.zeros((SUBLANES, tm), F32)
    chosen = jnp.zeros(lt.shape, F32)
    for kk in range(TOP_K):
        idx_out = jnp.where(row8 == kk, idxs[kk], idx_out)
        gate_out = jnp.where(row8 == kk, es[kk] / denom, gate_out)
        chosen = jnp.where(row == idxs[kk], 1.0, chosen)
    idx_ref[...] = idx_out
    gate_ref[...] = gate_out
    ones = jnp.ones((SUBLANES, tm), BF16)
    cnt = lax.dot_general(ones, chosen.astype(BF16), (((1,), (1,)), ((), ())),
                          preferred_element_type=F32)
    cnt_ref[0] = cnt.astype(jnp.int32)


def _router(h, w_router, b_router):
    t, d = h.shape
    tm = min(MOE_TILE, t)
    nt = t // tm
    w = jnp.zeros((d, ROUTER_PAD), F32).at[:, :N_EXPERTS].set(w_router)
    b = jnp.zeros((1, ROUTER_PAD), F32).at[0, :N_EXPERTS].set(b_router)
    idx, gate, cnt = pl.pallas_call(
        _router_kernel,
        grid=(nt,),
        in_specs=[
            pl.BlockSpec((tm, d), lambda i: (i, 0)),
            pl.BlockSpec((d, ROUTER_PAD), lambda i: (0, 0)),
            pl.BlockSpec((1, ROUTER_PAD), lambda i: (0, 0)),
        ],
        out_specs=[
            pl.BlockSpec((SUBLANES, tm), lambda i: (0, i)),
            pl.BlockSpec((SUBLANES, tm), lambda i: (0, i)),
            pl.BlockSpec((1, SUBLANES, ROUTER_PAD), lambda i: (i, 0, 0)),
        ],
        out_shape=[
            jax.ShapeDtypeStruct((SUBLANES, t), jnp.int32),
            jax.ShapeDtypeStruct((SUBLANES, t), F32),
            jax.ShapeDtypeStruct((nt, SUBLANES, ROUTER_PAD), jnp.int32),
        ],
        compiler_params=_cparams("parallel"),
        name="router",
    )(h, w, b)
    return idx, gate, cnt[:, 0, :N_EXPERTS]


def _moe_plan(cnt, n_rows):
    a, blk = ROW_ALIGN, MOE_ROWS
    cnt_a = _round_up(cnt, a)
    rows_e = jnp.sum(cnt_a, 0)
    cap_e = _round_up(rows_e, blk)
    seg_end = jnp.cumsum(cap_e)
    seg_start = seg_end - cap_e
    gbase = seg_start[None, :] + jnp.cumsum(cnt_a, 0) - cnt_a
    loff = jnp.cumsum(cnt_a, 1) - cnt_a
    nch = cnt_a // a
    nb = n_rows // blk
    n_used = seg_end[-1] // blk
    block_row = jnp.minimum(jnp.arange(nb, dtype=jnp.int32), n_used - 1) * blk
    block_e = jnp.minimum(jnp.sum(seg_end[None, :] <= block_row[:, None], 1), N_EXPERTS - 1)
    loffv = jnp.zeros((cnt.shape[0], ROUTER_PAD, 1), F32).at[:, :N_EXPERTS, 0].set(loff.astype(F32))
    i32 = lambda x: x.astype(jnp.int32)
    return dict(nch=i32(nch), gbase=i32(gbase), loff=i32(loff), ntot=i32(jnp.sum(nch, 1)),
                pad_start=i32(seg_start + rows_e), pad_n=i32((cap_e - rows_e) // a),
                block_e=i32(block_e), n_used=i32(n_used).reshape(1), loffv=loffv)


def _local_positions(idx, loffv, u):
    tm = idx.shape[1]
    row = lax.broadcasted_iota(jnp.int32, (ROUTER_PAD, tm), 0)
    hits = [row == idx[kk:kk + 1, :] for kk in range(TOP_K)]
    chosen = jnp.zeros((ROUTER_PAD, tm), F32)
    for hit in hits:
        chosen = jnp.where(hit, 1.0, chosen)
    before = jnp.dot(chosen.astype(BF16), u, preferred_element_type=F32)
    pos = before + loffv
    return [jnp.sum(jnp.where(hit, pos, 0.0), 0, keepdims=True).astype(jnp.int32) for hit in hits]


def _selection_matrix(js, vals, n_local, tm):
    rr = lax.broadcasted_iota(jnp.int32, (n_local, tm), 0)
    out = jnp.zeros((n_local, tm), F32)
    for j, v in zip(js, vals):
        out = jnp.where(rr == j, v, out)
    return out


def _run_copies(nch_s, src_s, dst_s, i, copy):
    big = RUN_BATCH * ROW_ALIGN

    def per_expert(e, carry):
        src0 = src_s[i, e]
        dst0 = dst_s[i, e]
        n = nch_s[i, e]
        n_big = lax.shift_right_logical(n, jnp.int32(RUN_BATCH.bit_length() - 1))

        def big_copy(c, carry2):
            copy(pl.multiple_of(src0 + c * big, ROW_ALIGN),
                 pl.multiple_of(dst0 + c * big, ROW_ALIGN), big).start()
            return carry2

        def small_copy(c, carry2):
            off = n_big * big + c * ROW_ALIGN
            copy(pl.multiple_of(src0 + off, ROW_ALIGN),
                 pl.multiple_of(dst0 + off, ROW_ALIGN)).start()
            return carry2

        lax.fori_loop(0, n_big, big_copy, 0)
        lax.fori_loop(0, n & (RUN_BATCH - 1), small_copy, 0)
        return carry

    lax.fori_loop(0, N_EXPERTS, per_expert, 0)


def _drain(n, copy):
    def many(c, carry):
        copy(0, 0, DRAIN_BATCH * ROW_ALIGN).wait()
        return carry

    def one(c, carry):
        copy(0, 0).wait()
        return carry

    lax.fori_loop(0, lax.shift_right_logical(n, jnp.int32(DRAIN_BATCH.bit_length() - 1)), many, 0)
    lax.fori_loop(0, n & (DRAIN_BATCH - 1), one, 0)


def _dispatch_kernel(nch_s, gbase_s, loff_s, ntot_s, pstart_s, pn_s, nu_s,
                     h_ref, idx_ref, loffv_ref, u_ref, xs_hbm, loc_ref, zero_ref, sem, zsem):
    i = pl.program_id(0)
    n_local, tm = loc_ref.shape[1], h_ref.shape[0]
    slot = i % 2
    js = _local_positions(idx_ref[...], loffv_ref[0], u_ref[...])
    perm = _selection_matrix(js, [1.0] * TOP_K, n_local, tm).astype(BF16)
    loc_ref[slot] = jnp.dot(perm, h_ref[...].astype(BF16), preferred_element_type=F32)

    def slot_copy(which):
        def copy(src_row, dst_row, rows=ROW_ALIGN):
            return pltpu.make_async_copy(loc_ref.at[which, pl.ds(src_row, rows)],
                                         xs_hbm.at[pl.ds(dst_row, rows)], sem.at[which])
        return copy

    def zero_copy(src_row, dst_row, rows=ROW_ALIGN):
        return pltpu.make_async_copy(zero_ref.at[pl.ds(src_row, rows)],
                                     xs_hbm.at[pl.ds(dst_row, rows)], zsem)

    def zero_block_copy(block):
        return pltpu.make_async_copy(
            zero_ref, xs_hbm.at[pl.ds(pl.multiple_of(block * MOE_ROWS, MOE_ROWS), MOE_ROWS)], zsem)

    _run_copies(nch_s, loff_s, gbase_s, i, slot_copy(slot))

    @pl.when(i > 0)
    def _():
        _drain(ntot_s[jnp.maximum(i - 1, 0)], slot_copy(1 - slot))

    @pl.when(i == pl.num_programs(0) - 1)
    def _():
        _drain(ntot_s[i], slot_copy(slot))
        zero_ref[...] = jnp.zeros_like(zero_ref)

        def per_expert(e, total):
            def per_chunk(c, carry):
                zero_copy(0, pl.multiple_of(pstart_s[e] + c * ROW_ALIGN, ROW_ALIGN)).start()
                return carry

            lax.fori_loop(0, pn_s[e], per_chunk, 0)
            return total + pn_s[e]

        total = lax.fori_loop(0, N_EXPERTS, per_expert, 0)
        _drain(total, zero_copy)
        n_blocks = xs_hbm.shape[0] // MOE_ROWS

        def start_block(blk, carry):
            zero_block_copy(blk).start()
            return carry

        def wait_block(blk, carry):
            zero_block_copy(blk).wait()
            return carry

        lax.fori_loop(nu_s[0], n_blocks, start_block, 0)
        lax.fori_loop(nu_s[0], n_blocks, wait_block, 0)


def _dispatch(h, idx, plan, u, n_rows):
    t, d = h.shape
    tm = min(MOE_TILE, t)
    n_local = _round_up(TOP_K * tm + N_EXPERTS * (ROW_ALIGN - 1), 2 * LANES)
    grid_spec = pltpu.PrefetchScalarGridSpec(
        num_scalar_prefetch=7,
        grid=(t // tm,),
        in_specs=[
            pl.BlockSpec((tm, d), lambda i, *_: (i, 0)),
            pl.BlockSpec((SUBLANES, tm), lambda i, *_: (0, i)),
            pl.BlockSpec((1, ROUTER_PAD, 1), lambda i, *_: (i, 0, 0)),
            pl.BlockSpec((tm, tm), lambda i, *_: (0, 0)),
        ],
        out_specs=pl.BlockSpec(memory_space=pl.ANY),
        scratch_shapes=[
            pltpu.VMEM((2, n_local, d), F32),
            pltpu.VMEM((MOE_ROWS, d), F32),
            pltpu.SemaphoreType.DMA((2,)),
            pltpu.SemaphoreType.DMA,
        ],
    )
    return pl.pallas_call(
        _dispatch_kernel,
        grid_spec=grid_spec,
        out_shape=jax.ShapeDtypeStruct((n_rows, d), F32),
        compiler_params=_cparams("arbitrary"),
        name="dispatch",
    )(plan["nch"], plan["gbase"], plan["loff"], plan["ntot"], plan["pad_start"], plan["pad_n"],
      plan["n_used"], h, idx, plan["loffv"], u)


def _expert_kernel(be_ref, nu_ref, x_ref, wu_ref, bu_ref, wd_ref, bd_ref, o_ref, wu_bf, wd_bf):
    i = pl.program_id(0)

    @pl.when(jnp.logical_or(i == 0, be_ref[i] != be_ref[jnp.maximum(i - 1, 0)]))
    def _():
        wu_bf[...] = wu_ref[0, 0].astype(BF16)
        wd_bf[...] = wd_ref[0, 0].astype(BF16)

    @pl.when(i < nu_ref[0])
    def _():
        hdn = jnp.dot(x_ref[...].astype(BF16), wu_bf[...], preferred_element_type=F32) + bu_ref[0, 0]
        glu = jnp.minimum(hdn[:, :D_FF], SWIGLU_LIMIT)
        lin = jnp.clip(hdn[:, D_FF:], -SWIGLU_LIMIT, SWIGLU_LIMIT)
        act = glu * jax.nn.sigmoid(SWIGLU_ALPHA * glu) * (lin + 1.0)
        o_ref[...] = jnp.dot(act.astype(BF16), wd_bf[...], preferred_element_type=F32) + bd_ref[0, 0]

    @pl.when(i >= nu_ref[0])
    def _():
        o_ref[...] = jnp.zeros_like(o_ref)


def _experts(xs, plan, w_up, b_up, w_down, b_down, layer):
    n_rows, d = xs.shape
    depth, e = w_up.shape[:2]
    used = lambda i, nu: jnp.maximum(jnp.minimum(i, nu[0] - 1), 0)
    grid_spec = pltpu.PrefetchScalarGridSpec(
        num_scalar_prefetch=2,
        grid=(n_rows // MOE_ROWS,),
        in_specs=[
            pl.BlockSpec((MOE_ROWS, d), lambda i, be, nu: (used(i, nu), 0)),
            pl.BlockSpec((1, 1, d, 2 * D_FF), lambda i, be, nu: (layer, be[i], 0, 0)),
            pl.BlockSpec((1, 1, 1, 2 * D_FF), lambda i, be, nu: (layer, be[i], 0, 0)),
            pl.BlockSpec((1, 1, D_FF, d), lambda i, be, nu: (layer, be[i], 0, 0)),
            pl.BlockSpec((1, 1, 1, d), lambda i, be, nu: (layer, be[i], 0, 0)),
        ],
        out_specs=pl.BlockSpec((MOE_ROWS, d), lambda i, be, nu: (i, 0)),
        scratch_shapes=[pltpu.VMEM((d, 2 * D_FF), BF16), pltpu.VMEM((D_FF, d), BF16)],
    )
    return pl.pallas_call(
        _expert_kernel,
        grid_spec=grid_spec,
        out_shape=jax.ShapeDtypeStruct((n_rows, d), F32),
        compiler_params=pltpu.CompilerParams(dimension_semantics=("arbitrary",),
                                             vmem_limit_bytes=EXPERT_VMEM_LIMIT),
        name="experts",
    )(plan["block_e"], plan["n_used"], xs, w_up, b_up.reshape(depth, e, 1, 2 * D_FF), w_down,
      b_down.reshape(depth, e, 1, d))


def _combine_kernel(nch_s, gbase_s, loff_s, ntot_s,
                    h_ref, idx_ref, gate_ref, loffv_ref, u_ref, g_ref, b_ref, ys_hbm,
                    o_ref, loc_ref, sem):
    i = pl.program_id(0)
    n_local, tm = loc_ref.shape[1], h_ref.shape[0]
    slot = i % 2

    def slot_copy(which):
        def copy(dst_row, src_row, rows=ROW_ALIGN):
            return pltpu.make_async_copy(ys_hbm.at[pl.ds(src_row, rows)],
                                         loc_ref.at[which, pl.ds(dst_row, rows)], sem.at[which])
        return copy

    @pl.when(i == 0)
    def _():
        loc_ref[...] = jnp.zeros_like(loc_ref)
        _run_copies(nch_s, loff_s, gbase_s, 0, slot_copy(0))

    @pl.when(i + 1 < pl.num_programs(0))
    def _():
        _run_copies(nch_s, loff_s, gbase_s, i + 1, slot_copy(1 - slot))

    js = _local_positions(idx_ref[...], loffv_ref[0], u_ref[...])
    gates = gate_ref[...]
    row = lax.broadcasted_iota(jnp.int32, (LANES, tm), 0)
    stacked = jnp.zeros((LANES, tm), F32)
    for kk in range(TOP_K):
        stacked = jnp.where(row == kk, js[kk].astype(F32), stacked)
        stacked = jnp.where(row == TOP_K + kk, gates[kk:kk + 1, :], stacked)
    cols = stacked.T
    rr = lax.broadcasted_iota(jnp.int32, (tm, n_local), 1)
    gmat = jnp.zeros((tm, n_local), F32)
    for kk in range(TOP_K):
        gmat = jnp.where(rr == cols[:, kk:kk + 1].astype(jnp.int32),
                         cols[:, TOP_K + kk:TOP_K + kk + 1], gmat)
    _drain(ntot_s[i], slot_copy(slot))
    ffn = jnp.dot(gmat.astype(BF16), loc_ref[slot].astype(BF16), preferred_element_type=F32)
    o_ref[...] = _layer_norm(DN_ALPHA * h_ref[...] + ffn, g_ref[...], b_ref[...])


def _combine_ln(h, ys, idx, gate, plan, u, g, b):
    t, d = h.shape
    tm = min(MOE_TILE, t)
    n_local = _round_up(TOP_K * tm + N_EXPERTS * (ROW_ALIGN - 1), 2 * LANES)
    grid_spec = pltpu.PrefetchScalarGridSpec(
        num_scalar_prefetch=4,
        grid=(t // tm,),
        in_specs=[
            pl.BlockSpec((tm, d), lambda i, *_: (i, 0)),
            pl.BlockSpec((SUBLANES, tm), lambda i, *_: (0, i)),
            pl.BlockSpec((SUBLANES, tm), lambda i, *_: (0, i)),
            pl.BlockSpec((1, ROUTER_PAD, 1), lambda i, *_: (i, 0, 0)),
            pl.BlockSpec((tm, tm), lambda i, *_: (0, 0)),
            pl.BlockSpec((1, d), lambda i, *_: (0, 0)),
            pl.BlockSpec((1, d), lambda i, *_: (0, 0)),
            pl.BlockSpec(memory_space=pl.ANY),
        ],
        out_specs=pl.BlockSpec((tm, d), lambda i, *_: (i, 0)),
        scratch_shapes=[pltpu.VMEM((2, n_local, d), F32), pltpu.SemaphoreType.DMA((2,))],
    )
    return pl.pallas_call(
        _combine_kernel,
        grid_spec=grid_spec,
        out_shape=jax.ShapeDtypeStruct((t, d), F32),
        compiler_params=_cparams("arbitrary"),
        name="combine_ln",
    )(plan["nch"], plan["gbase"], plan["loff"], plan["ntot"],
      h, idx, gate, plan["loffv"], u, g.reshape(1, d), b.reshape(1, d), ys)


def _moe_ln(h, w_router, b_router, w_up, b_up, w_down, b_down, layer, ln_g, ln_b):
    t, d = h.shape
    tm = min(MOE_TILE, t)
    nt = t // tm
    n_rows = _round_up(t * TOP_K + nt * N_EXPERTS * (ROW_ALIGN - 1) + N_EXPERTS * (MOE_ROWS - 1),
                       MOE_ROWS)
    idx, gate, cnt = _router(h, w_router, b_router)
    plan = _moe_plan(cnt, n_rows)
    tri = jnp.arange(tm)
    u = (tri[:, None] < tri[None, :]).astype(BF16)
    xs = _dispatch(h, idx, plan, u, n_rows)
    ys = _experts(xs, plan, w_up, b_up, w_down, b_down, layer)
    return _combine_ln(h, ys, idx, gate, plan, u, ln_g, ln_b)


def _ret_rope_tables(seq):
    inv_freq = 1.0 / (RET_ROPE_THETA ** jnp.linspace(0.0, 1.0, RET_QK_DIM // 2, dtype=F32))
    ang = jnp.arange(seq, dtype=F32)[:, None] * inv_freq[None, :]
    cos, sin = jnp.cos(ang), jnp.sin(ang)
    return jnp.concatenate([cos, cos], -1), jnp.concatenate([-sin, sin], -1)


def _partial_rope_tables(seq):
    inv_freq = 1.0 / (ROPE_THETA ** (jnp.arange(0, ROPE_DIM, 2, dtype=F32) / ROPE_DIM))
    ang = jnp.arange(seq, dtype=F32)[:, None] * inv_freq[None, :]
    cos, sin = jnp.cos(ang), jnp.sin(ang)
    pad = jnp.zeros((seq, DIFF_HEAD_DIM - ROPE_DIM), F32)
    c64 = jnp.concatenate([cos, cos, pad + 1.0], -1)
    s64 = jnp.concatenate([-sin, sin, pad], -1)
    return jnp.concatenate([c64, c64], -1), jnp.concatenate([s64, s64], -1)


def kernel(x, ret_w_in, ret_w_out, kv_w, diff_w_q, diff_lambda, diff_subln_g, diff_w_out,
           ln_attn_g, ln_attn_b, ln_ffn_g, ln_ffn_b, moe_w_router, moe_b_router,
           moe_w_up, moe_b_up, moe_w_down, moe_b_down):
    batch, seq, d = x.shape
    t = batch * seq
    h = x.reshape(t, d)
    ret_cos, ret_sin = _ret_rope_tables(seq)
    par_cos, par_sin = _partial_rope_tables(seq)
    qk_scale = jnp.concatenate([jnp.ones((1, d), F32), jnp.full((1, d), RET_QK_DIM ** -0.5, F32)], -1)
    k_sh = v_sh = None
    for l in range(DEPTH):
        if l < N_A_LAYERS:
            w_in = ret_w_in[l].astype(BF16)
            qk = _proj(h, w_in[:, :2 * d], BF16, rope="half", cos=ret_cos, sin=ret_sin,
                       scale=qk_scale, seq=seq)
            v = _proj(h, w_in[:, 2 * d:2 * d + RET_V_WIDTH], BF16)
            g = _proj(h, w_in[:, 2 * d + RET_V_WIDTH:], F32)
            a = _retention(qk, v, g, batch, seq)
            h = _outproj_ln(a, ret_w_out[l].astype(BF16), h, ln_attn_g[l], ln_attn_b[l])
        else:
            j = l - N_A_LAYERS
            lambda_init = 0.8 - 0.6 * math.exp(-0.3 * l)
            q = _proj(h, diff_w_q[j].astype(BF16), BF16, rope="partial", cos=par_cos, sin=par_sin,
                      scale=jnp.full((1, d), DIFF_HEAD_DIM ** -0.5 * math.log2(math.e), F32), seq=seq)
            a = _diff_attention(q, k_sh, v_sh, diff_lambda[j], diff_subln_g[j], lambda_init, batch, seq)
            h = _outproj_ln(a, diff_w_out[j].astype(BF16), h, ln_attn_g[l], ln_attn_b[l])
        h = _moe_ln(h, moe_w_router[l], moe_b_router[l], moe_w_up, moe_b_up, moe_w_down, moe_b_down,
                    l, ln_ffn_g[l], ln_ffn_b[l])
        if l == N_A_LAYERS - 1:
            kvw = kv_w.astype(BF16)
            k_sh = _proj(h, kvw[:, :d], BF16, rope="partial", cos=par_cos, sin=par_sin, seq=seq)
            v_sh = _transpose_values(_proj(h, kvw[:, d:], BF16), batch, seq)
    return h.reshape(batch, seq, d)
```

```python
import functools
import math

import jax
import jax.numpy as jnp
from jax import lax
from jax.experimental import pallas as pl
from jax.experimental.pallas import tpu as pltpu

F32 = jnp.float32
BF16 = jnp.bfloat16

D_MODEL = 1024
DEPTH = 4
N_A_LAYERS = DEPTH // 2
RET_HEADS = 8
RET_QK_DIM = D_MODEL // RET_HEADS
RET_V_DIM = 2 * RET_QK_DIM
RET_V_WIDTH = RET_HEADS * RET_V_DIM
RET_ROPE_THETA = 10000.0
DIFF_HEAD_DIM = 64
DIFF_HEADS = D_MODEL // (2 * DIFF_HEAD_DIM)
DIFF_V_DIM = 2 * DIFF_HEAD_DIM
ROPE_THETA = 500000.0
ROPE_DIM = DIFF_HEAD_DIM // 4
NEG_BIG = -1e30
N_EXPERTS = 32
TOP_K = 4
D_FF = D_MODEL
SWIGLU_ALPHA = 1.702
SWIGLU_LIMIT = 7.0
DN_ALPHA = (2 * DEPTH) ** 0.25
LN_EPS = 1e-5

LANES = 128
SUBLANES = 8
VMEM_LIMIT = 48 * 1024 * 1024
EXPERT_VMEM_LIMIT = 58 * 1024 * 1024

ROW_TILE = 512
PROJ_TILE = 1024
RET_CHUNK = 256
ATTN_TILE = 512
ATTN_GROUP = 256
MOE_ROWS = 512
MOE_TILE = 512
DISPATCH_CHUNK = 768
COMBINE_CHUNK = 256
RUN_BATCH = 4
DRAIN_BATCH = 8
ROW_ALIGN = SUBLANES
ROUTER_PAD = LANES


def _cparams(*sem):
    return pltpu.CompilerParams(dimension_semantics=sem, vmem_limit_bytes=VMEM_LIMIT)


def _round_up(x, m):
    return (x + m - 1) // m * m


def _layer_norm(y, g, b):
    mu = jnp.mean(y, -1, keepdims=True)
    d = y - mu
    var = jnp.mean(d * d, -1, keepdims=True)
    return d * lax.rsqrt(var + LN_EPS) * g + b


def _proj_kernel(x_ref, w_ref, c_ref, s_ref, scale_ref, o_ref, *, rope):
    acc = jnp.dot(x_ref[...].astype(BF16), w_ref[...], preferred_element_type=F32)
    tn = acc.shape[1]
    if rope is None:
        o_ref[...] = acc.astype(o_ref.dtype)
        return
    c = c_ref[...]
    s = s_ref[...]
    lane = lax.broadcasted_iota(jnp.int32, (1, LANES), 1)
    for g in range(tn // LANES):
        sl = slice(g * LANES, (g + 1) * LANES)
        xg = acc[:, sl]
        if rope == "half":
            y = xg * c + pltpu.roll(xg, LANES // 2, 1) * s
        else:
            half = ROPE_DIM // 2
            up = pltpu.roll(xg, LANES - half, 1)
            dn = pltpu.roll(xg, half, 1)
            partner = jnp.where((lane & half) == 0, up, dn)
            y = jnp.where((lane & (DIFF_HEAD_DIM - 1)) < ROPE_DIM, xg * c + partner * s, xg)
        o_ref[:, sl] = (y * scale_ref[:, sl]).astype(o_ref.dtype)


def _proj(x, w, out_dtype, *, rope=None, cos=None, sin=None, scale=None, seq=None, tn=1024):
    t, k = x.shape
    n = w.shape[1]
    tm = min(PROJ_TILE, t, seq or t)
    tn = min(tn, n)
    if cos is None:
        cos = jnp.zeros((tm, LANES), F32)
        sin = cos
        pos_blocks = 1
    else:
        pos_blocks = seq // tm
    if scale is None:
        scale = jnp.ones((1, n), F32)
    return pl.pallas_call(
        functools.partial(_proj_kernel, rope=rope),
        grid=(n // tn, t // tm),
        in_specs=[
            pl.BlockSpec((tm, k), lambda j, i: (i, 0)),
            pl.BlockSpec((k, tn), lambda j, i: (0, j)),
            pl.BlockSpec((tm, LANES), lambda j, i: (i % pos_blocks, 0)),
            pl.BlockSpec((tm, LANES), lambda j, i: (i % pos_blocks, 0)),
            pl.BlockSpec((1, tn), lambda j, i: (0, j)),
        ],
        out_specs=pl.BlockSpec((tm, tn), lambda j, i: (i, j)),
        out_shape=jax.ShapeDtypeStruct((t, n), out_dtype),
        compiler_params=_cparams("parallel", "parallel"),
        name="proj",
    )(x, w, cos, sin, scale)


def _outproj_ln_kernel(a_ref, w_ref, h_ref, g_ref, b_ref, o_ref):
    mix = jnp.dot(a_ref[...], w_ref[...], preferred_element_type=F32)
    o_ref[...] = _layer_norm(DN_ALPHA * h_ref[...] + mix, g_ref[...], b_ref[...])


def _outproj_ln(a, w, h, g, b):
    t, k = a.shape
    d = w.shape[1]
    tm = min(ROW_TILE, t)
    return pl.pallas_call(
        _outproj_ln_kernel,
        grid=(t // tm,),
        in_specs=[
            pl.BlockSpec((tm, k), lambda i: (i, 0)),
            pl.BlockSpec((k, d), lambda i: (0, 0)),
            pl.BlockSpec((tm, d), lambda i: (i, 0)),
            pl.BlockSpec((1, d), lambda i: (0, 0)),
            pl.BlockSpec((1, d), lambda i: (0, 0)),
        ],
        out_specs=pl.BlockSpec((tm, d), lambda i: (i, 0)),
        out_shape=jax.ShapeDtypeStruct((t, d), F32),
        compiler_params=_cparams("parallel"),
        name="outproj_ln",
    )(a, w, h, g.reshape(1, d), b.reshape(1, d))


def _retention_kernel(q_ref, k_ref, v_ref, g_ref, dm_ref, qd_ref, kd_ref, cd_ref, o_ref, state_ref):
    @pl.when(pl.program_id(1) == 0)
    def _():
        state_ref[...] = jnp.zeros_like(state_ref)

    dk, dv = RET_QK_DIM, RET_V_DIM
    heads = range(RET_HEADS)
    q = [q_ref[:, h * dk:(h + 1) * dk] for h in heads]
    k = [k_ref[:, h * dk:(h + 1) * dk] for h in heads]
    v = [v_ref[:, h * dv:(h + 1) * dv] for h in heads]
    s = [lax.dot_general(q[h], k[h], (((1,), (1,)), ((), ())), preferred_element_type=F32)
         for h in heads]
    state = [state_ref[h] for h in heads]
    cross = [jnp.dot(q[h], state[h].astype(BF16), preferred_element_type=F32) for h in heads]
    kdec = [(k[h].astype(F32) * kd_ref[h]).astype(BF16) for h in heads]
    kv = [lax.dot_general(kdec[h], v[h], (((0,), (0,)), ((), ())), preferred_element_type=F32)
          for h in heads]
    inner = [jnp.dot((s[h] * dm_ref[h]).astype(BF16), v[h], preferred_element_type=F32)
             for h in heads]
    for h in heads:
        state_ref[h] = cd_ref[h] * state[h] + kv[h]
        o = inner[h] + cross[h] * qd_ref[h]
        mu = jnp.mean(o, -1, keepdims=True)
        d = o - mu
        var = jnp.mean(d * d, -1, keepdims=True)
        gate = g_ref[:, h * dv:(h + 1) * dv]
        o_ref[:, h * dv:(h + 1) * dv] = (d * lax.rsqrt(var + LN_EPS)
                                         * (gate * jax.nn.sigmoid(gate))).astype(o_ref.dtype)


def _retention_tables(c):
    h = RET_HEADS
    log_gamma = jnp.log1p(-jnp.exp2(-5.0 - jnp.arange(h, dtype=F32)))
    idx = jnp.arange(c, dtype=F32)
    rel = idx[:, None] - idx[None, :]
    dmask = jnp.where(rel >= 0, jnp.exp(jnp.maximum(rel, 0.0) * log_gamma[:, None, None]), 0.0)
    k_decay = jnp.exp((c - 1 - idx) * log_gamma[:, None])[:, :, None]
    q_decay = jnp.exp((idx + 1) * log_gamma[:, None])[:, :, None]
    chunk_decay = jnp.exp(c * log_gamma)[:, None, None]
    return dmask, q_decay, k_decay, chunk_decay


def _retention(qk, v, g, batch, seq):
    t = qk.shape[0]
    c = min(RET_CHUNK, seq)
    n = seq // c
    h, dk, dv = RET_HEADS, RET_QK_DIM, RET_V_DIM
    dmask, q_decay, k_decay, chunk_decay = _retention_tables(c)
    row = lambda b, nn: b * n + nn
    whole = lambda b, nn: (0, 0, 0)
    return pl.pallas_call(
        _retention_kernel,
        grid=(batch, n),
        in_specs=[
            pl.BlockSpec((c, h * dk), lambda b, nn: (row(b, nn), 0)),
            pl.BlockSpec((c, h * dk), lambda b, nn: (row(b, nn), 1)),
            pl.BlockSpec((c, h * dv), lambda b, nn: (row(b, nn), 0)),
            pl.BlockSpec((c, h * dv), lambda b, nn: (row(b, nn), 0)),
            pl.BlockSpec((h, c, c), whole),
            pl.BlockSpec((h, c, 1), whole),
            pl.BlockSpec((h, c, 1), whole),
            pl.BlockSpec((h, 1, 1), whole),
        ],
        out_specs=pl.BlockSpec((c, h * dv), lambda b, nn: (row(b, nn), 0)),
        out_shape=jax.ShapeDtypeStruct((t, h * dv), BF16),
        scratch_shapes=[pltpu.VMEM((h, dk, dv), F32)],
        compiler_params=_cparams("parallel", "arbitrary"),
        name="retention",
    )(qk, qk, v, g, dmask, q_decay, k_decay, chunk_decay)


def _diff_attn_kernel(q_ref, k_ref, vt_ref, lam_ref, sg_ref, o_ref, qm_ref, m_ref, l_ref, acc_ref,
                      s0_ref, s1_ref, *, lambda_init):
    tk = vt_ref.shape[2]
    tq = tk
    gw = min(ATTN_GROUP, tq)
    groups = [slice(g * gw, (g + 1) * gw) for g in range(2 * tq // gw)]
    lam = lam_ref[...]
    lam_full = (jnp.exp(jnp.sum(lam[0:1] * lam[1:2], -1, keepdims=True))
                - jnp.exp(jnp.sum(lam[2:3] * lam[3:4], -1, keepdims=True)) + lambda_init)

    def score_tile(s_ref, j):
        kt = k_ref[pl.ds(pl.multiple_of(j * tk, tk), tk), :]
        for cols in groups:
            s_ref[:, cols] = lax.dot_general(kt, qm_ref[cols, :], (((1,), (1,)), ((), ())),
                                             preferred_element_type=F32)

    def consume(s_ref, j, masked):
        vt = vt_ref[j]
        for g, cols in enumerate(groups):
            s = s_ref[:, cols]
            if masked:
                kpos = lax.broadcasted_iota(jnp.int32, s.shape, 0)
                qpos = (lax.broadcasted_iota(jnp.int32, s.shape, 1) + g * gw) & (tq - 1)
                s = jnp.where(kpos <= qpos, s, NEG_BIG)
            m_old = m_ref[:, cols]
            m_new = jnp.maximum(m_old, jnp.max(s, 0, keepdims=True))
            alpha = jnp.exp2(m_old - m_new)
            p = jnp.exp2(s - m_new)
            l_ref[:, cols] = alpha * l_ref[:, cols] + jnp.sum(p, 0, keepdims=True)
            acc_ref[:, cols] = alpha * acc_ref[:, cols] + jnp.dot(vt, p.astype(BF16),
                                                                  preferred_element_type=F32)
            m_ref[:, cols] = m_new

    def query_tile(i, carry):
        rows = pl.ds(pl.multiple_of(i * tq, tq), tq)
        q = q_ref[rows, :]
        lane = lax.broadcasted_iota(jnp.int32, q.shape, 1)
        zero = jnp.zeros_like(q)
        qm_ref[:tq, :] = jnp.where(lane < DIFF_HEAD_DIM, q, zero)
        qm_ref[tq:, :] = jnp.where(lane >= DIFF_HEAD_DIM, q, zero)
        m_ref[...] = jnp.full_like(m_ref, NEG_BIG)
        l_ref[...] = jnp.zeros_like(l_ref)
        acc_ref[...] = jnp.zeros_like(acc_ref)

        score_tile(s0_ref, 0)

        def pair(jj, carry2):
            j = 2 * jj
            score_tile(s1_ref, j + 1)
            consume(s0_ref, j, False)
            score_tile(s0_ref, j + 2)
            consume(s1_ref, j + 1, False)
            return carry2

        lax.fori_loop(0, i // 2, pair, 0)

        @pl.when(i % 2 == 0)
        def _():
            consume(s0_ref, i, True)

        @pl.when(i % 2 == 1)
        def _():
            score_tile(s1_ref, i)
            consume(s0_ref, i - 1, False)
            consume(s1_ref, i, True)

        o1 = acc_ref[:, :tq] / l_ref[:, :tq]
        o2 = acc_ref[:, tq:] / l_ref[:, tq:]
        o = o1 - lam_full * o2
        o = (o * lax.rsqrt(jnp.mean(o * o, 0, keepdims=True) + LN_EPS) * sg_ref[...]
             * (1.0 - lambda_init))
        o_ref[rows, :] = o.T.astype(o_ref.dtype)
        return carry

    lax.fori_loop(0, q_ref.shape[0] // tq, query_tile, 0)


def _diff_attention(q, k, vt, lam, subln_g, lambda_init, batch, seq):
    t, d = q.shape
    tq = min(ATTN_TILE, seq)
    nq = seq // tq
    hd = 2 * DIFF_HEAD_DIM
    return pl.pallas_call(
        functools.partial(_diff_attn_kernel, lambda_init=lambda_init),
        grid=(batch, DIFF_HEADS),
        in_specs=[
            pl.BlockSpec((seq, hd), lambda b, h: (b, h)),
            pl.BlockSpec((seq, hd), lambda b, h: (b, h)),
            pl.BlockSpec((nq, hd, tq), lambda b, h: (b * DIFF_HEADS + h, 0, 0)),
            pl.BlockSpec((4, DIFF_HEAD_DIM), lambda b, h: (0, 0)),
            pl.BlockSpec((hd, 1), lambda b, h: (0, 0)),
        ],
        out_specs=pl.BlockSpec((seq, hd), lambda b, h: (b, h)),
        out_shape=jax.ShapeDtypeStruct((t, d), BF16),
        scratch_shapes=[
            pltpu.VMEM((2 * tq, hd), BF16),
            pltpu.VMEM((1, 2 * tq), F32),
            pltpu.VMEM((1, 2 * tq), F32),
            pltpu.VMEM((hd, 2 * tq), F32),
            pltpu.VMEM((tq, 2 * tq), F32),
            pltpu.VMEM((tq, 2 * tq), F32),
        ],
        compiler_params=_cparams("parallel", "parallel"),
        name="diff_attn",
    )(q, k, vt, lam, subln_g.reshape(hd, 1))


def _transpose_values(v, batch, seq):
    tk = min(ATTN_TILE, seq)
    nk = seq // tk
    hd = 2 * DIFF_HEAD_DIM
    v5 = v.reshape(batch, nk, tk, DIFF_HEADS, hd).transpose(0, 3, 1, 4, 2)
    return v5.reshape(batch * DIFF_HEADS * nk, hd, tk)


def _split_bf16(x):
    hi = x.astype(BF16)
    return hi, (x - hi.astype(F32)).astype(BF16)


def _router_kernel(h_ref, w_ref, b_ref, idx_ref, gate_ref, cnt_ref):
    h_hi, h_lo = _split_bf16(h_ref[...])
    w_hi, w_lo = _split_bf16(w_ref[...])
    logits = (jnp.dot(h_hi, w_hi, preferred_element_type=F32)
              + jnp.dot(h_lo, w_hi, preferred_element_type=F32)
              + jnp.dot(h_hi, w_lo, preferred_element_type=F32)) + b_ref[...]
    lt = logits.T
    tm = lt.shape[1]
    row = lax.broadcasted_iota(jnp.int32, lt.shape, 0)
    cur = jnp.where(row < N_EXPERTS, lt, -jnp.inf)
    vals, idxs = [], []
    for _ in range(TOP_K):
        m = jnp.max(cur, 0, keepdims=True)
        sel = jnp.min(jnp.where(cur == m, row, ROUTER_PAD), 0, keepdims=True)
        vals.append(m)
        idxs.append(sel)
        cur = jnp.where(row == sel, -jnp.inf, cur)
    es = [jnp.exp(vk - vals[0]) for vk in vals]
    denom = es[0] + es[1] + es[2] + es[3]
    row8 = lax.broadcasted_iota(jnp.int32, (SUBLANES, tm), 0)
    idx_out = jnp.full((SUBLANES, tm), -1, jnp.int32)
    gate_out = jnp.zeros((SUBLANES, tm), F32)
    chosen = jnp.zeros(lt.shape, F32)
    for kk in range(TOP_K):
        idx_out = jnp.where(row8 == kk, idxs[kk], idx_out)
        gate_out = jnp.where(row8 == kk, es[kk] / denom, gate_out)
        chosen = jnp.where(row == idxs[kk], 1.0, chosen)
    idx_ref[...] = idx_out
    gate_ref[...] = gate_out
    ones = jnp.ones((SUBLANES, tm), BF16)
    cnt = lax.dot_general(ones, chosen.astype(BF16), (((1,), (1,)), ((), ())),
                          preferred_element_type=F32)
    cnt_ref[0] = cnt.astype(jnp.int32)


def _router(h, w_router, b_router):
    t, d = h.shape
    tm = min(MOE_TILE, t)
    nt = t // tm
    w = jnp.zeros((d, ROUTER_PAD), F32).at[:, :N_EXPERTS].set(w_router)
    b = jnp.zeros((1, ROUTER_PAD), F32).at[0, :N_EXPERTS].set(b_router)
    idx, gate, cnt = pl.pallas_call(
        _router_kernel,
        grid=(nt,),
        in_specs=[
            pl.BlockSpec((tm, d), lambda i: (i, 0)),
            pl.BlockSpec((d, ROUTER_PAD), lambda i: (0, 0)),
            pl.BlockSpec((1, ROUTER_PAD), lambda i: (0, 0)),
        ],
        out_specs=[
            pl.BlockSpec((SUBLANES, tm), lambda i: (0, i)),
            pl.BlockSpec((SUBLANES, tm), lambda i: (0, i)),
            pl.BlockSpec((1, SUBLANES, ROUTER_PAD), lambda i: (i, 0, 0)),
        ],
        out_shape=[
            jax.ShapeDtypeStruct((SUBLANES, t), jnp.int32),
            jax.ShapeDtypeStruct((SUBLANES, t), F32),
            jax.ShapeDtypeStruct((nt, SUBLANES, ROUTER_PAD), jnp.int32),
        ],
        compiler_params=_cparams("parallel"),
        name="router",
    )(h, w, b)
    return idx, gate, cnt[:, 0, :N_EXPERTS]


def _moe_plan(cnt, n_rows):
    a, blk = ROW_ALIGN, MOE_ROWS
    cnt_a = _round_up(cnt, a)
    rows_e = jnp.sum(cnt_a, 0)
    cap_e = _round_up(rows_e, blk)
    seg_end = jnp.cumsum(cap_e)
    seg_start = seg_end - cap_e
    gbase = seg_start[None, :] + jnp.cumsum(cnt_a, 0) - cnt_a
    loff = jnp.cumsum(cnt_a, 1) - cnt_a
    nch = cnt_a // a
    nb = n_rows // blk
    n_used = seg_end[-1] // blk
    block_row = jnp.minimum(jnp.arange(nb, dtype=jnp.int32), n_used - 1) * blk
    block_e = jnp.minimum(jnp.sum(seg_end[None, :] <= block_row[:, None], 1), N_EXPERTS - 1)
    loffv = jnp.zeros((cnt.shape[0], ROUTER_PAD, 1), F32).at[:, :N_EXPERTS, 0].set(loff.astype(F32))
    i32 = lambda x: x.astype(jnp.int32)
    return dict(nch=i32(nch), gbase=i32(gbase), loff=i32(loff), ntot=i32(jnp.sum(nch, 1)),
                pad_start=i32(seg_start + rows_e), pad_n=i32((cap_e - rows_e) // a),
                block_e=i32(block_e), n_used=i32(n_used).reshape(1), loffv=loffv)


def _local_positions(idx, loffv, u):
    tm = idx.shape[1]
    row = lax.broadcasted_iota(jnp.int32, (ROUTER_PAD, tm), 0)
    hits = [row == idx[kk:kk + 1, :] for kk in range(TOP_K)]
    chosen = jnp.zeros((ROUTER_PAD, tm), F32)
    for hit in hits:
        chosen = jnp.where(hit, 1.0, chosen)
    before = jnp.dot(chosen.astype(BF16), u, preferred_element_type=F32)
    pos = before + loffv
    return [jnp.sum(jnp.where(hit, pos, 0.0), 0, keepdims=True).astype(jnp.int32) for hit in hits]


def _selection_matrix(js, vals, n_local, tm):
    rr = lax.broadcasted_iota(jnp.int32, (n_local, tm), 0)
    out = jnp.zeros((n_local, tm), F32)
    for j, v in zip(js, vals):
        out = jnp.where(rr == j, v, out)
    return out


def _run_copies(nch_s, src_s, dst_s, i, copy):
    big = RUN_BATCH * ROW_ALIGN

    def per_expert(e, carry):
        src0 = src_s[i, e]
        dst0 = dst_s[i, e]
        n = nch_s[i, e]
        n_big = lax.shift_right_logical(n, jnp.int32(RUN_BATCH.bit_length() - 1))

        def big_copy(c, carry2):
            copy(pl.multiple_of(src0 + c * big, ROW_ALIGN),
                 pl.multiple_of(dst0 + c * big, ROW_ALIGN), big).start()
            return carry2

        def small_copy(c, carry2):
            off = n_big * big + c * ROW_ALIGN
            copy(pl.multiple_of(src0 + off, ROW_ALIGN),
                 pl.multiple_of(dst0 + off, ROW_ALIGN)).start()
            return carry2

        lax.fori_loop(0, n_big, big_copy, 0)
        lax.fori_loop(0, n & (RUN_BATCH - 1), small_copy, 0)
        return carry

    lax.fori_loop(0, N_EXPERTS, per_expert, 0)


def _drain(n, copy):
    def many(c, carry):
        copy(0, 0, DRAIN_BATCH * ROW_ALIGN).wait()
        return carry

    def one(c, carry):
        copy(0, 0).wait()
        return carry

    lax.fori_loop(0, lax.shift_right_logical(n, jnp.int32(DRAIN_BATCH.bit_length() - 1)), many, 0)
    lax.fori_loop(0, n & (DRAIN_BATCH - 1), one, 0)


def _dispatch_kernel(nch_s, gbase_s, loff_s, ntot_s, pstart_s, pn_s, nu_s,
                     h_ref, idx_ref, loffv_ref, u_ref, xs_hbm, loc_ref, zero_ref, sem, zsem):
    i = pl.program_id(0)
    n_local, tm = loc_ref.shape[1], h_ref.shape[0]
    slot = i % 2
    js = _local_positions(idx_ref[...], loffv_ref[0], u_ref[...])
    x = h_ref[...].astype(BF16)
    for c in range(n_local // DISPATCH_CHUNK):
        rr = lax.broadcasted_iota(jnp.int32, (DISPATCH_CHUNK, tm), 0) + c * DISPATCH_CHUNK
        perm = jnp.zeros((DISPATCH_CHUNK, tm), F32)
        for j in js:
            perm = jnp.where(rr == j, 1.0, perm)
        loc_ref[slot, c * DISPATCH_CHUNK:(c + 1) * DISPATCH_CHUNK, :] = jnp.dot(
            perm.astype(BF16), x, preferred_element_type=F32)

    def slot_copy(which):
        def copy(src_row, dst_row, rows=ROW_ALIGN):
            return pltpu.make_async_copy(loc_ref.at[which, pl.ds(src_row, rows)],
                                         xs_hbm.at[pl.ds(dst_row, rows)], sem.at[which])
        return copy

    def zero_copy(src_row, dst_row, rows=ROW_ALIGN):
        return pltpu.make_async_copy(zero_ref.at[pl.ds(src_row, rows)],
                                     xs_hbm.at[pl.ds(dst_row, rows)], zsem)

    def zero_block_copy(block):
        return pltpu.make_async_copy(
            zero_ref, xs_hbm.at[pl.ds(pl.multiple_of(block * MOE_ROWS, MOE_ROWS), MOE_ROWS)], zsem)

    _run_copies(nch_s, loff_s, gbase_s, i, slot_copy(slot))

    @pl.when(i > 0)
    def _():
        _drain(ntot_s[jnp.maximum(i - 1, 0)], slot_copy(1 - slot))

    @pl.when(i == pl.num_programs(0) - 1)
    def _():
        _drain(ntot_s[i], slot_copy(slot))
        zero_ref[...] = jnp.zeros_like(zero_ref)

        def per_expert(e, total):
            def per_chunk(c, carry):
                zero_copy(0, pl.multiple_of(pstart_s[e] + c * ROW_ALIGN, ROW_ALIGN)).start()
                return carry

            lax.fori_loop(0, pn_s[e], per_chunk, 0)
            return total + pn_s[e]

        total = lax.fori_loop(0, N_EXPERTS, per_expert, 0)
        _drain(total, zero_copy)
        n_blocks = xs_hbm.shape[0] // MOE_ROWS

        def start_block(blk, carry):
            zero_block_copy(blk).start()
            return carry

        def wait_block(blk, carry):
            zero_block_copy(blk).wait()
            return carry

        lax.fori_loop(nu_s[0], n_blocks, start_block, 0)
        lax.fori_loop(nu_s[0], n_blocks, wait_block, 0)


def _local_rows(tm):
    n_local = _round_up(TOP_K * tm + N_EXPERTS * (ROW_ALIGN - 1), 2 * LANES)
    assert n_local % DISPATCH_CHUNK == 0 and n_local % COMBINE_CHUNK == 0, n_local
    return n_local


def _dispatch(h, idx, plan, u, n_rows):
    t, d = h.shape
    tm = min(MOE_TILE, t)
    n_local = _local_rows(tm)
    grid_spec = pltpu.PrefetchScalarGridSpec(
        num_scalar_prefetch=7,
        grid=(t // tm,),
        in_specs=[
            pl.BlockSpec((tm, d), lambda i, *_: (i, 0)),
            pl.BlockSpec((SUBLANES, tm), lambda i, *_: (0, i)),
            pl.BlockSpec((1, ROUTER_PAD, 1), lambda i, *_: (i, 0, 0)),
            pl.BlockSpec((tm, tm), lambda i, *_: (0, 0)),
        ],
        out_specs=pl.BlockSpec(memory_space=pl.ANY),
        scratch_shapes=[
            pltpu.VMEM((2, n_local, d), F32),
            pltpu.VMEM((MOE_ROWS, d), F32),
            pltpu.SemaphoreType.DMA((2,)),
            pltpu.SemaphoreType.DMA,
        ],
    )
    return pl.pallas_call(
        _dispatch_kernel,
        grid_spec=grid_spec,
        out_shape=jax.ShapeDtypeStruct((n_rows, d), F32),
        compiler_params=_cparams("arbitrary"),
        name="dispatch",
    )(plan["nch"], plan["gbase"], plan["loff"], plan["ntot"], plan["pad_start"], plan["pad_n"],
      plan["n_used"], h, idx, plan["loffv"], u)


def _expert_kernel(be_ref, nu_ref, x_ref, wu_ref, bu_ref, wd_ref, bd_ref, o_ref, wu_bf, wd_bf):
    i = pl.program_id(0)

    @pl.when(jnp.logical_or(i == 0, be_ref[i] != be_ref[jnp.maximum(i - 1, 0)]))
    def _():
        wu_bf[...] = wu_ref[0, 0].astype(BF16)
        wd_bf[...] = wd_ref[0, 0].astype(BF16)

    @pl.when(i < nu_ref[0])
    def _():
        hdn = jnp.dot(x_ref[...].astype(BF16), wu_bf[...], preferred_element_type=F32) + bu_ref[0, 0]
        glu = jnp.minimum(hdn[:, :D_FF], SWIGLU_LIMIT)
        lin = jnp.clip(hdn[:, D_FF:], -SWIGLU_LIMIT, SWIGLU_LIMIT)
        act = glu * jax.nn.sigmoid(SWIGLU_ALPHA * glu) * (lin + 1.0)
        o_ref[...] = jnp.dot(act.astype(BF16), wd_bf[...], preferred_element_type=F32) + bd_ref[0, 0]

    @pl.when(i >= nu_ref[0])
    def _():
        o_ref[...] = jnp.zeros_like(o_ref)


def _experts(xs, plan, w_up, b_up, w_down, b_down, layer):
    n_rows, d = xs.shape
    depth, e = w_up.shape[:2]
    used = lambda i, nu: jnp.maximum(jnp.minimum(i, nu[0] - 1), 0)
    grid_spec = pltpu.PrefetchScalarGridSpec(
        num_scalar_prefetch=2,
        grid=(n_rows // MOE_ROWS,),
        in_specs=[
            pl.BlockSpec((MOE_ROWS, d), lambda i, be, nu: (used(i, nu), 0)),
            pl.BlockSpec((1, 1, d, 2 * D_FF), lambda i, be, nu: (layer, be[i], 0, 0)),
            pl.BlockSpec((1, 1, 1, 2 * D_FF), lambda i, be, nu: (layer, be[i], 0, 0)),
            pl.BlockSpec((1, 1, D_FF, d), lambda i, be, nu: (layer, be[i], 0, 0)),
            pl.BlockSpec((1, 1, 1, d), lambda i, be, nu: (layer, be[i], 0, 0)),
        ],
        out_specs=pl.BlockSpec((MOE_ROWS, d), lambda i, be, nu: (i, 0)),
        scratch_shapes=[pltpu.VMEM((d, 2 * D_FF), BF16), pltpu.VMEM((D_FF, d), BF16)],
    )
    return pl.pallas_call(
        _expert_kernel,
        grid_spec=grid_spec,
        out_shape=jax.ShapeDtypeStruct((n_rows, d), F32),
        compiler_params=pltpu.CompilerParams(dimension_semantics=("arbitrary",),
                                             vmem_limit_bytes=EXPERT_VMEM_LIMIT),
        name="experts",
    )(plan["block_e"], plan["n_used"], xs, w_up, b_up.reshape(depth, e, 1, 2 * D_FF), w_down,
      b_down.reshape(depth, e, 1, d))


def _combine_kernel(nch_s, gbase_s, loff_s, ntot_s,
                    h_ref, idx_ref, gate_ref, loffv_ref, u_ref, g_ref, b_ref, ys_hbm,
                    o_ref, loc_ref, sem):
    i = pl.program_id(0)
    n_local, tm = loc_ref.shape[1], h_ref.shape[0]
    slot = i % 2

    def slot_copy(which):
        def copy(dst_row, src_row, rows=ROW_ALIGN):
            return pltpu.make_async_copy(ys_hbm.at[pl.ds(src_row, rows)],
                                         loc_ref.at[which, pl.ds(dst_row, rows)], sem.at[which])
        return copy

    @pl.when(i == 0)
    def _():
        loc_ref[...] = jnp.zeros_like(loc_ref)
        _run_copies(nch_s, loff_s, gbase_s, 0, slot_copy(0))

    @pl.when(i + 1 < pl.num_programs(0))
    def _():
        _run_copies(nch_s, loff_s, gbase_s, i + 1, slot_copy(1 - slot))

    js = _local_positions(idx_ref[...], loffv_ref[0], u_ref[...])
    gates = gate_ref[...]
    row = lax.broadcasted_iota(jnp.int32, (LANES, tm), 0)
    stacked = jnp.zeros((LANES, tm), F32)
    for kk in range(TOP_K):
        stacked = jnp.where(row == kk, js[kk].astype(F32), stacked)
        stacked = jnp.where(row == TOP_K + kk, gates[kk:kk + 1, :], stacked)
    cols = stacked.T
    pos = [cols[:, kk:kk + 1].astype(jnp.int32) for kk in range(TOP_K)]
    gate = [cols[:, TOP_K + kk:TOP_K + kk + 1] for kk in range(TOP_K)]
    _drain(ntot_s[i], slot_copy(slot))
    ffn = jnp.zeros((tm, h_ref.shape[1]), F32)
    for c in range(n_local // COMBINE_CHUNK):
        rr = lax.broadcasted_iota(jnp.int32, (tm, COMBINE_CHUNK), 1) + c * COMBINE_CHUNK
        gmat = jnp.zeros((tm, COMBINE_CHUNK), F32)
        for kk in range(TOP_K):
            gmat = jnp.where(rr == pos[kk], gate[kk], gmat)
        rows = loc_ref[slot, c * COMBINE_CHUNK:(c + 1) * COMBINE_CHUNK, :]
        ffn = ffn + jnp.dot(gmat.astype(BF16), rows.astype(BF16), preferred_element_type=F32)
    o_ref[...] = _layer_norm(DN_ALPHA * h_ref[...] + ffn, g_ref[...], b_ref[...])


def _combine_ln(h, ys, idx, gate, plan, u, g, b):
    t, d = h.shape
    tm = min(MOE_TILE, t)
    n_local = _local_rows(tm)
    grid_spec = pltpu.PrefetchScalarGridSpec(
        num_scalar_prefetch=4,
        grid=(t // tm,),
        in_specs=[
            pl.BlockSpec((tm, d), lambda i, *_: (i, 0)),
            pl.BlockSpec((SUBLANES, tm), lambda i, *_: (0, i)),
            pl.BlockSpec((SUBLANES, tm), lambda i, *_: (0, i)),
            pl.BlockSpec((1, ROUTER_PAD, 1), lambda i, *_: (i, 0, 0)),
            pl.BlockSpec((tm, tm), lambda i, *_: (0, 0)),
            pl.BlockSpec((1, d), lambda i, *_: (0, 0)),
            pl.BlockSpec((1, d), lambda i, *_: (0, 0)),
            pl.BlockSpec(memory_space=pl.ANY),
        ],
        out_specs=pl.BlockSpec((tm, d), lambda i, *_: (i, 0)),
        scratch_shapes=[pltpu.VMEM((2, n_local, d), F32), pltpu.SemaphoreType.DMA((2,))],
    )
    return pl.pallas_call(
        _combine_kernel,
        grid_spec=grid_spec,
        out_shape=jax.ShapeDtypeStruct((t, d), F32),
        compiler_params=_cparams("arbitrary"),
        name="combine_ln",
    )(plan["nch"], plan["gbase"], plan["loff"], plan["ntot"],
      h, idx, gate, plan["loffv"], u, g.reshape(1, d), b.reshape(1, d), ys)


def _moe_ln(h, w_router, b_router, w_up, b_up, w_down, b_down, layer, ln_g, ln_b):
    t, d = h.shape
    tm = min(MOE_TILE, t)
    nt = t // tm
    n_rows = _round_up(t * TOP_K + nt * N_EXPERTS * (ROW_ALIGN - 1) + N_EXPERTS * (MOE_ROWS - 1),
                       MOE_ROWS)
    idx, gate, cnt = _router(h, w_router, b_router)
    plan = _moe_plan(cnt, n_rows)
    tri = jnp.arange(tm)
    u = (tri[:, None] < tri[None, :]).astype(BF16)
    xs = _dispatch(h, idx, plan, u, n_rows)
    ys = _experts(xs, plan, w_up, b_up, w_down, b_down, layer)
    return _combine_ln(h, ys, idx, gate, plan, u, ln_g, ln_b)


def _ret_rope_tables(seq):
    inv_freq = 1.0 / (RET_ROPE_THETA ** jnp.linspace(0.0, 1.0, RET_QK_DIM // 2, dtype=F32))
    ang = jnp.arange(seq, dtype=F32)[:, None] * inv_freq[None, :]
    cos, sin = jnp.cos(ang), jnp.sin(ang)
    return jnp.concatenate([cos, cos], -1), jnp.concatenate([-sin, sin], -1)


def _partial_rope_tables(seq):
    inv_freq = 1.0 / (ROPE_THETA ** (jnp.arange(0, ROPE_DIM, 2, dtype=F32) / ROPE_DIM))
    ang = jnp.arange(seq, dtype=F32)[:, None] * inv_freq[None, :]
    cos, sin = jnp.cos(ang), jnp.sin(ang)
    pad = jnp.zeros((seq, DIFF_HEAD_DIM - ROPE_DIM), F32)
    c64 = jnp.concatenate([cos, cos, pad + 1.0], -1)
    s64 = jnp.concatenate([-sin, sin, pad], -1)
    return jnp.concatenate([c64, c64], -1), jnp.concatenate([s64, s64], -1)


def kernel(x, ret_w_in, ret_w_out, kv_w, diff_w_q, diff_lambda, diff_subln_g, diff_w_out,
           ln_attn_g, ln_attn_b, ln_ffn_g, ln_ffn_b, moe_w_router, moe_b_router,
           moe_w_up, moe_b_up, moe_w_down, moe_b_down):
    batch, seq, d = x.shape
    t = batch * seq
    h = x.reshape(t, d)
    ret_cos, ret_sin = _ret_rope_tables(seq)
    par_cos, par_sin = _partial_rope_tables(seq)
    qk_scale = jnp.concatenate([jnp.ones((1, d), F32), jnp.full((1, d), RET_QK_DIM ** -0.5, F32)], -1)
    k_sh = v_sh = None
    for l in range(DEPTH):
        if l < N_A_LAYERS:
            w_in = ret_w_in[l].astype(BF16)
            qk = _proj(h, w_in[:, :2 * d], BF16, rope="half", cos=ret_cos, sin=ret_sin,
                       scale=qk_scale, seq=seq)
            v = _proj(h, w_in[:, 2 * d:2 * d + RET_V_WIDTH], BF16)
            g = _proj(h, w_in[:, 2 * d + RET_V_WIDTH:], F32)
            a = _retention(qk, v, g, batch, seq)
            h = _outproj_ln(a, ret_w_out[l].astype(BF16), h, ln_attn_g[l], ln_attn_b[l])
        else:
            j = l - N_A_LAYERS
            lambda_init = 0.8 - 0.6 * math.exp(-0.3 * l)
            q = _proj(h, diff_w_q[j].astype(BF16), BF16, rope="partial", cos=par_cos, sin=par_sin,
                      scale=jnp.full((1, d), DIFF_HEAD_DIM ** -0.5 * math.log2(math.e), F32), seq=seq)
            a = _diff_attention(q, k_sh, v_sh, diff_lambda[j], diff_subln_g[j], lambda_init, batch, seq)
            h = _outproj_ln(a, diff_w_out[j].astype(BF16), h, ln_attn_g[l], ln_attn_b[l])
        h = _moe_ln(h, moe_w_router[l], moe_b_router[l], moe_w_up, moe_b_up, moe_w_down, moe_b_down,
                    l, ln_ffn_g[l], ln_ffn_b[l])
        if l == N_A_LAYERS - 1:
            kvw = kv_w.astype(BF16)
            k_sh = _proj(h, kvw[:, :d], BF16, rope="partial", cos=par_cos, sin=par_sin, seq=seq)
            v_sh = _transpose_values(_proj(h, kvw[:, d:], BF16), batch, seq)
    return h.reshape(batch, seq, d)
```

```python
import functools
import math

import jax
import jax.numpy as jnp
from jax import lax
from jax.experimental import pallas as pl
from jax.experimental.pallas import tpu as pltpu

F32 = jnp.float32
BF16 = jnp.bfloat16

D_MODEL = 1024
DEPTH = 4
N_A_LAYERS = DEPTH // 2
RET_HEADS = 8
RET_QK_DIM = D_MODEL // RET_HEADS
RET_V_DIM = 2 * RET_QK_DIM
RET_V_WIDTH = RET_HEADS * RET_V_DIM
RET_ROPE_THETA = 10000.0
DIFF_HEAD_DIM = 64
DIFF_HEADS = D_MODEL // (2 * DIFF_HEAD_DIM)
DIFF_V_DIM = 2 * DIFF_HEAD_DIM
ROPE_THETA = 500000.0
ROPE_DIM = DIFF_HEAD_DIM // 4
NEG_BIG = -1e30
N_EXPERTS = 32
TOP_K = 4
D_FF = D_MODEL
SWIGLU_ALPHA = 1.702
SWIGLU_LIMIT = 7.0
DN_ALPHA = (2 * DEPTH) ** 0.25
LN_EPS = 1e-5

LANES = 128
SUBLANES = 8
VMEM_LIMIT = 48 * 1024 * 1024
EXPERT_VMEM_LIMIT = 58 * 1024 * 1024

ROW_TILE = 512
PROJ_TILE = 1024
PROJ_COLS_BF16 = 2048
PROJ_COLS_F32 = 1024
RET_CHUNK = 256
ATTN_TILE = 512
ATTN_GROUP = 256
MOE_ROWS = 512
MOE_TILE = 512
DISPATCH_CHUNK = 768
COMBINE_CHUNK = 256
RUN_BATCH = 4
DRAIN_BATCH = 8
ROW_ALIGN = SUBLANES
ROUTER_PAD = LANES


def _cparams(*sem):
    return pltpu.CompilerParams(dimension_semantics=sem, vmem_limit_bytes=VMEM_LIMIT)


def _round_up(x, m):
    return (x + m - 1) // m * m


def _layer_norm(y, g, b):
    mu = jnp.mean(y, -1, keepdims=True)
    d = y - mu
    var = jnp.mean(d * d, -1, keepdims=True)
    return d * lax.rsqrt(var + LN_EPS) * g + b


def _proj_kernel(x_ref, w_ref, c_ref, s_ref, scale_ref, o_ref, *, rope):
    acc = jnp.dot(x_ref[...].astype(BF16), w_ref[...], preferred_element_type=F32)
    tn = acc.shape[1]
    if rope is None:
        o_ref[...] = acc.astype(o_ref.dtype)
        return
    c = c_ref[...]
    s = s_ref[...]
    lane = lax.broadcasted_iota(jnp.int32, (1, LANES), 1)
    for g in range(tn // LANES):
        sl = slice(g * LANES, (g + 1) * LANES)
        xg = acc[:, sl]
        if rope == "half":
            y = xg * c + pltpu.roll(xg, LANES // 2, 1) * s
        else:
            half = ROPE_DIM // 2
            up = pltpu.roll(xg, LANES - half, 1)
            dn = pltpu.roll(xg, half, 1)
            partner = jnp.where((lane & half) == 0, up, dn)
            y = jnp.where((lane & (DIFF_HEAD_DIM - 1)) < ROPE_DIM, xg * c + partner * s, xg)
        o_ref[:, sl] = (y * scale_ref[:, sl]).astype(o_ref.dtype)


def _proj(x, w, out_dtype, *, rope=None, cos=None, sin=None, scale=None, seq=None):
    t, k = x.shape
    n = w.shape[1]
    tm = min(PROJ_TILE, t, seq or t)
    tn = min(PROJ_COLS_BF16 if jnp.dtype(out_dtype).itemsize == 2 else PROJ_COLS_F32, n)
    if cos is None:
        cos = jnp.zeros((tm, LANES), F32)
        sin = cos
        pos_blocks = 1
    else:
        pos_blocks = seq // tm
    if scale is None:
        scale = jnp.ones((1, n), F32)
    return pl.pallas_call(
        functools.partial(_proj_kernel, rope=rope),
        grid=(n // tn, t // tm),
        in_specs=[
            pl.BlockSpec((tm, k), lambda j, i: (i, 0)),
            pl.BlockSpec((k, tn), lambda j, i: (0, j)),
            pl.BlockSpec((tm, LANES), lambda j, i: (i % pos_blocks, 0)),
            pl.BlockSpec((tm, LANES), lambda j, i: (i % pos_blocks, 0)),
            pl.BlockSpec((1, tn), lambda j, i: (0, j)),
        ],
        out_specs=pl.BlockSpec((tm, tn), lambda j, i: (i, j)),
        out_shape=jax.ShapeDtypeStruct((t, n), out_dtype),
        compiler_params=_cparams("parallel", "parallel"),
        name="proj",
    )(x, w, cos, sin, scale)


def _outproj_ln_kernel(a_ref, w_ref, h_ref, g_ref, b_ref, o_ref):
    mix = jnp.dot(a_ref[...], w_ref[...], preferred_element_type=F32)
    o_ref[...] = _layer_norm(DN_ALPHA * h_ref[...] + mix, g_ref[...], b_ref[...])


def _outproj_ln(a, w, h, g, b):
    t, k = a.shape
    d = w.shape[1]
    tm = min(ROW_TILE, t)
    return pl.pallas_call(
        _outproj_ln_kernel,
        grid=(t // tm,),
        in_specs=[
            pl.BlockSpec((tm, k), lambda i: (i, 0)),
            pl.BlockSpec((k, d), lambda i: (0, 0)),
            pl.BlockSpec((tm, d), lambda i: (i, 0)),
            pl.BlockSpec((1, d), lambda i: (0, 0)),
            pl.BlockSpec((1, d), lambda i: (0, 0)),
        ],
        out_specs=pl.BlockSpec((tm, d), lambda i: (i, 0)),
        out_shape=jax.ShapeDtypeStruct((t, d), F32),
        compiler_params=_cparams("parallel"),
        name="outproj_ln",
    )(a, w, h, g.reshape(1, d), b.reshape(1, d))


def _retention_kernel(q_ref, k_ref, v_ref, g_ref, dm_ref, qd_ref, kd_ref, cd_ref, o_ref, state_ref):
    @pl.when(pl.program_id(1) == 0)
    def _():
        state_ref[...] = jnp.zeros_like(state_ref)

    dk, dv = RET_QK_DIM, RET_V_DIM
    heads = range(RET_HEADS)
    q = [q_ref[:, h * dk:(h + 1) * dk] for h in heads]
    k = [k_ref[:, h * dk:(h + 1) * dk] for h in heads]
    v = [v_ref[:, h * dv:(h + 1) * dv] for h in heads]
    s = [lax.dot_general(q[h], k[h], (((1,), (1,)), ((), ())), preferred_element_type=F32)
         for h in heads]
    state = [state_ref[h] for h in heads]
    cross = [jnp.dot(q[h], state[h].astype(BF16), preferred_element_type=F32) for h in heads]
    kdec = [(k[h].astype(F32) * kd_ref[h]).astype(BF16) for h in heads]
    kv = [lax.dot_general(kdec[h], v[h], (((0,), (0,)), ((), ())), preferred_element_type=F32)
          for h in heads]
    inner = [jnp.dot((s[h] * dm_ref[h]).astype(BF16), v[h], preferred_element_type=F32)
             for h in heads]
    for h in heads:
        state_ref[h] = cd_ref[h] * state[h] + kv[h]
        o = inner[h] + cross[h] * qd_ref[h]
        mu = jnp.mean(o, -1, keepdims=True)
        d = o - mu
        var = jnp.mean(d * d, -1, keepdims=True)
        gate = g_ref[:, h * dv:(h + 1) * dv]
        o_ref[:, h * dv:(h + 1) * dv] = (d * lax.rsqrt(var + LN_EPS)
                                         * (gate * jax.nn.sigmoid(gate))).astype(o_ref.dtype)


def _retention_tables(c):
    h = RET_HEADS
    log_gamma = jnp.log1p(-jnp.exp2(-5.0 - jnp.arange(h, dtype=F32)))
    idx = jnp.arange(c, dtype=F32)
    rel = idx[:, None] - idx[None, :]
    dmask = jnp.where(rel >= 0, jnp.exp(jnp.maximum(rel, 0.0) * log_gamma[:, None, None]), 0.0)
    k_decay = jnp.exp((c - 1 - idx) * log_gamma[:, None])[:, :, None]
    q_decay = jnp.exp((idx + 1) * log_gamma[:, None])[:, :, None]
    chunk_decay = jnp.exp(c * log_gamma)[:, None, None]
    return dmask, q_decay, k_decay, chunk_decay


def _retention(qk, v, g, batch, seq):
    t = qk.shape[0]
    c = min(RET_CHUNK, seq)
    n = seq // c
    h, dk, dv = RET_HEADS, RET_QK_DIM, RET_V_DIM
    dmask, q_decay, k_decay, chunk_decay = _retention_tables(c)
    row = lambda b, nn: b * n + nn
    whole = lambda b, nn: (0, 0, 0)
    return pl.pallas_call(
        _retention_kernel,
        grid=(batch, n),
        in_specs=[
            pl.BlockSpec((c, h * dk), lambda b, nn: (row(b, nn), 0)),
            pl.BlockSpec((c, h * dk), lambda b, nn: (row(b, nn), 1)),
            pl.BlockSpec((c, h * dv), lambda b, nn: (row(b, nn), 0)),
            pl.BlockSpec((c, h * dv), lambda b, nn: (row(b, nn), 0)),
            pl.BlockSpec((h, c, c), whole),
            pl.BlockSpec((h, c, 1), whole),
            pl.BlockSpec((h, c, 1), whole),
            pl.BlockSpec((h, 1, 1), whole),
        ],
        out_specs=pl.BlockSpec((c, h * dv), lambda b, nn: (row(b, nn), 0)),
        out_shape=jax.ShapeDtypeStruct((t, h * dv), BF16),
        scratch_shapes=[pltpu.VMEM((h, dk, dv), F32)],
        compiler_params=_cparams("parallel", "arbitrary"),
        name="retention",
    )(qk, qk, v, g, dmask, q_decay, k_decay, chunk_decay)


def _diff_attn_kernel(q_ref, k_ref, vt_ref, lam_ref, sg_ref, o_ref, qm_ref, m_ref, l_ref, acc_ref,
                      s0_ref, s1_ref, *, lambda_init):
    tk = vt_ref.shape[2]
    tq = tk
    gw = min(ATTN_GROUP, tq)
    groups = [slice(g * gw, (g + 1) * gw) for g in range(2 * tq // gw)]
    lam = lam_ref[...]
    lam_full = (jnp.exp(jnp.sum(lam[0:1] * lam[1:2], -1, keepdims=True))
                - jnp.exp(jnp.sum(lam[2:3] * lam[3:4], -1, keepdims=True)) + lambda_init)

    def score_tile(s_ref, j):
        kt = k_ref[pl.ds(pl.multiple_of(j * tk, tk), tk), :]
        for cols in groups:
            s_ref[:, cols] = lax.dot_general(kt, qm_ref[cols, :], (((1,), (1,)), ((), ())),
                                             preferred_element_type=F32)

    def consume(s_ref, j, masked):
        vt = vt_ref[j]
        for g, cols in enumerate(groups):
            s = s_ref[:, cols]
            if masked:
                kpos = lax.broadcasted_iota(jnp.int32, s.shape, 0)
                qpos = (lax.broadcasted_iota(jnp.int32, s.shape, 1) + g * gw) & (tq - 1)
                s = jnp.where(kpos <= qpos, s, NEG_BIG)
            m_old = m_ref[:, cols]
            m_new = jnp.maximum(m_old, jnp.max(s, 0, keepdims=True))
            alpha = jnp.exp2(m_old - m_new)
            p = jnp.exp2(s - m_new)
            l_ref[:, cols] = alpha * l_ref[:, cols] + jnp.sum(p, 0, keepdims=True)
            acc_ref[:, cols] = alpha * acc_ref[:, cols] + jnp.dot(vt, p.astype(BF16),
                                                                  preferred_element_type=F32)
            m_ref[:, cols] = m_new

    def query_tile(i, carry):
        rows = pl.ds(pl.multiple_of(i * tq, tq), tq)
        q = q_ref[rows, :]
        lane = lax.broadcasted_iota(jnp.int32, q.shape, 1)
        zero = jnp.zeros_like(q)
        qm_ref[:tq, :] = jnp.where(lane < DIFF_HEAD_DIM, q, zero)
        qm_ref[tq:, :] = jnp.where(lane >= DIFF_HEAD_DIM, q, zero)
        m_ref[...] = jnp.full_like(m_ref, NEG_BIG)
        l_ref[...] = jnp.zeros_like(l_ref)
        acc_ref[...] = jnp.zeros_like(acc_ref)

        score_tile(s0_ref, 0)

        def pair(jj, carry2):
            j = 2 * jj
            score_tile(s1_ref, j + 1)
            consume(s0_ref, j, False)
            score_tile(s0_ref, j + 2)
            consume(s1_ref, j + 1, False)
            return carry2

        lax.fori_loop(0, i // 2, pair, 0)

        @pl.when(i % 2 == 0)
        def _():
            consume(s0_ref, i, True)

        @pl.when(i % 2 == 1)
        def _():
            score_tile(s1_ref, i)
            consume(s0_ref, i - 1, False)
            consume(s1_ref, i, True)

        o1 = acc_ref[:, :tq] / l_ref[:, :tq]
        o2 = acc_ref[:, tq:] / l_ref[:, tq:]
        o = o1 - lam_full * o2
        o = (o * lax.rsqrt(jnp.mean(o * o, 0, keepdims=True) + LN_EPS) * sg_ref[...]
             * (1.0 - lambda_init))
        o_ref[rows, :] = o.T.astype(o_ref.dtype)
        return carry

    lax.fori_loop(0, q_ref.shape[0] // tq, query_tile, 0)


def _diff_attention(q, k, vt, lam, subln_g, lambda_init, batch, seq):
    t, d = q.shape
    tq = min(ATTN_TILE, seq)
    nq = seq // tq
    hd = 2 * DIFF_HEAD_DIM
    return pl.pallas_call(
        functools.partial(_diff_attn_kernel, lambda_init=lambda_init),
        grid=(batch, DIFF_HEADS),
        in_specs=[
            pl.BlockSpec((seq, hd), lambda b, h: (b, h)),
            pl.BlockSpec((seq, hd), lambda b, h: (b, h)),
            pl.BlockSpec((nq, hd, tq), lambda b, h: (b * DIFF_HEADS + h, 0, 0)),
            pl.BlockSpec((4, DIFF_HEAD_DIM), lambda b, h: (0, 0)),
            pl.BlockSpec((hd, 1), lambda b, h: (0, 0)),
        ],
        out_specs=pl.BlockSpec((seq, hd), lambda b, h: (b, h)),
        out_shape=jax.ShapeDtypeStruct((t, d), BF16),
        scratch_shapes=[
            pltpu.VMEM((2 * tq, hd), BF16),
            pltpu.VMEM((1, 2 * tq), F32),
            pltpu.VMEM((1, 2 * tq), F32),
            pltpu.VMEM((hd, 2 * tq), F32),
            pltpu.VMEM((tq, 2 * tq), F32),
            pltpu.VMEM((tq, 2 * tq), F32),
        ],
        compiler_params=_cparams("parallel", "parallel"),
        name="diff_attn",
    )(q, k, vt, lam, subln_g.reshape(hd, 1))


def _transpose_values(v, batch, seq):
    tk = min(ATTN_TILE, seq)
    nk = seq // tk
    hd = 2 * DIFF_HEAD_DIM
    v5 = v.reshape(batch, nk, tk, DIFF_HEADS, hd).transpose(0, 3, 1, 4, 2)
    return v5.reshape(batch * DIFF_HEADS * nk, hd, tk)


def _split_bf16(x):
    hi = x.astype(BF16)
    return hi, (x - hi.astype(F32)).astype(BF16)


def _router_kernel(h_ref, w_ref, b_ref, idx_ref, gate_ref, cnt_ref):
    h_hi, h_lo = _split_bf16(h_ref[...])
    w_hi, w_lo = _split_bf16(w_ref[...])
    logits = (jnp.dot(h_hi, w_hi, preferred_element_type=F32)
              + jnp.dot(h_lo, w_hi, preferred_element_type=F32)
              + jnp.dot(h_hi, w_lo, preferred_element_type=F32)) + b_ref[...]
    lt = logits.T
    tm = lt.shape[1]
    row = lax.broadcasted_iota(jnp.int32, lt.shape, 0)
    cur = jnp.where(row < N_EXPERTS, lt, -jnp.inf)
    vals, idxs = [], []
    for _ in range(TOP_K):
        m = jnp.max(cur, 0, keepdims=True)
        sel = jnp.min(jnp.where(cur == m, row, ROUTER_PAD), 0, keepdims=True)
        vals.append(m)
        idxs.append(sel)
        cur = jnp.where(row == sel, -jnp.inf, cur)
    es = [jnp.exp(vk - vals[0]) for vk in vals]
    denom = es[0] + es[1] + es[2] + es[3]
    row8 = lax.broadcasted_iota(jnp.int32, (SUBLANES, tm), 0)
    idx_out = jnp.full((SUBLANES, tm), -1, jnp.int32)
    gate_out = jnp.zeros((SUBLANES, tm), F32)
    chosen = jnp.zeros(lt.shape, F32)
    for kk in range(TOP_K):
        idx_out = jnp.where(row8 == kk, idxs[kk], idx_out)
        gate_out = jnp.where(row8 == kk, es[kk] / denom, gate_out)
        chosen = jnp.where(row == idxs[kk], 1.0, chosen)
    idx_ref[...] = idx_out
    gate_ref[...] = gate_out
    ones = jnp.ones((SUBLANES, tm), BF16)
    cnt = lax.dot_general(ones, chosen.astype(BF16), (((1,), (1,)), ((), ())),
                          preferred_element_type=F32)
    cnt_ref[0] = cnt.astype(jnp.int32)


def _router(h, w_router, b_router):
    t, d = h.shape
    tm = min(MOE_TILE, t)
    nt = t // tm
    w = jnp.zeros((d, ROUTER_PAD), F32).at[:, :N_EXPERTS].set(w_router)
    b = jnp.zeros((1, ROUTER_PAD), F32).at[0, :N_EXPERTS].set(b_router)
    idx, gate, cnt = pl.pallas_call(
        _router_kernel,
        grid=(nt,),
        in_specs=[
            pl.BlockSpec((tm, d), lambda i: (i, 0)),
            pl.BlockSpec((d, ROUTER_PAD), lambda i: (0, 0)),
            pl.BlockSpec((1, ROUTER_PAD), lambda i: (0, 0)),
        ],
        out_specs=[
            pl.BlockSpec((SUBLANES, tm), lambda i: (0, i)),
            pl.BlockSpec((SUBLANES, tm), lambda i: (0, i)),
            pl.BlockSpec((1, SUBLANES, ROUTER_PAD), lambda i: (i, 0, 0)),
        ],
        out_shape=[
            jax.ShapeDtypeStruct((SUBLANES, t), jnp.int32),
            jax.ShapeDtypeStruct((SUBLANES, t), F32),
            jax.ShapeDtypeStruct((nt, SUBLANES, ROUTER_PAD), jnp.int32),
        ],
        compiler_params=_cparams("parallel"),
        name="router",
    )(h, w, b)
    return idx, gate, cnt[:, 0, :N_EXPERTS]


def _moe_plan(cnt, n_rows):
    a, blk = ROW_ALIGN, MOE_ROWS
    cnt_a = _round_up(cnt, a)
    rows_e = jnp.sum(cnt_a, 0)
    cap_e = _round_up(rows_e, blk)
    seg_end = jnp.cumsum(cap_e)
    seg_start = seg_end - cap_e
    gbase = seg_start[None, :] + jnp.cumsum(cnt_a, 0) - cnt_a
    loff = jnp.cumsum(cnt_a, 1) - cnt_a
    nch = cnt_a // a
    nb = n_rows // blk
    n_used = seg_end[-1] // blk
    block_row = jnp.minimum(jnp.arange(nb, dtype=jnp.int32), n_used - 1) * blk
    block_e = jnp.minimum(jnp.sum(seg_end[None, :] <= block_row[:, None], 1), N_EXPERTS - 1)
    loffv = jnp.zeros((cnt.shape[0], ROUTER_PAD, 1), F32).at[:, :N_EXPERTS, 0].set(loff.astype(F32))
    i32 = lambda x: x.astype(jnp.int32)
    return dict(nch=i32(nch), gbase=i32(gbase), loff=i32(loff), ntot=i32(jnp.sum(nch, 1)),
                pad_start=i32(seg_start + rows_e), pad_n=i32((cap_e - rows_e) // a),
                block_e=i32(block_e), n_used=i32(n_used).reshape(1), loffv=loffv)


def _local_positions(idx, loffv, u):
    tm = idx.shape[1]
    row = lax.broadcasted_iota(jnp.int32, (ROUTER_PAD, tm), 0)
    hits = [row == idx[kk:kk + 1, :] for kk in range(TOP_K)]
    chosen = jnp.zeros((ROUTER_PAD, tm), F32)
    for hit in hits:
        chosen = jnp.where(hit, 1.0, chosen)
    before = jnp.dot(chosen.astype(BF16), u, preferred_element_type=F32)
    pos = before + loffv
    return [jnp.sum(jnp.where(hit, pos, 0.0), 0, keepdims=True).astype(jnp.int32) for hit in hits]


def _selection_matrix(js, vals, n_local, tm):
    rr = lax.broadcasted_iota(jnp.int32, (n_local, tm), 0)
    out = jnp.zeros((n_local, tm), F32)
    for j, v in zip(js, vals):
        out = jnp.where(rr == j, v, out)
    return out


def _run_copies(nch_s, src_s, dst_s, i, copy):
    big = RUN_BATCH * ROW_ALIGN

    def per_expert(e, carry):
        src0 = src_s[i, e]
        dst0 = dst_s[i, e]
        n = nch_s[i, e]
        n_big = lax.shift_right_logical(n, jnp.int32(RUN_BATCH.bit_length() - 1))

        def big_copy(c, carry2):
            copy(pl.multiple_of(src0 + c * big, ROW_ALIGN),
                 pl.multiple_of(dst0 + c * big, ROW_ALIGN), big).start()
            return carry2

        def small_copy(c, carry2):
            off = n_big * big + c * ROW_ALIGN
            copy(pl.multiple_of(src0 + off, ROW_ALIGN),
                 pl.multiple_of(dst0 + off, ROW_ALIGN)).start()
            return carry2

        lax.fori_loop(0, n_big, big_copy, 0)
        lax.fori_loop(0, n & (RUN_BATCH - 1), small_copy, 0)
        return carry

    lax.fori_loop(0, N_EXPERTS, per_expert, 0)


def _drain(n, copy):
    def many(c, carry):
        copy(0, 0, DRAIN_BATCH * ROW_ALIGN).wait()
        return carry

    def one(c, carry):
        copy(0, 0).wait()
        return carry

    lax.fori_loop(0, lax.shift_right_logical(n, jnp.int32(DRAIN_BATCH.bit_length() - 1)), many, 0)
    lax.fori_loop(0, n & (DRAIN_BATCH - 1), one, 0)


def _dispatch_kernel(nch_s, gbase_s, loff_s, ntot_s, pstart_s, pn_s, nu_s,
                     h_ref, idx_ref, loffv_ref, u_ref, xs_hbm, loc_ref, zero_ref, sem, zsem):
    i = pl.program_id(0)
    n_local, tm = loc_ref.shape[1], h_ref.shape[0]
    slot = i % 2
    js = _local_positions(idx_ref[...], loffv_ref[0], u_ref[...])
    x = h_ref[...].astype(BF16)
    for c in range(n_local // DISPATCH_CHUNK):
        rr = lax.broadcasted_iota(jnp.int32, (DISPATCH_CHUNK, tm), 0) + c * DISPATCH_CHUNK
        perm = jnp.zeros((DISPATCH_CHUNK, tm), F32)
        for j in js:
            perm = jnp.where(rr == j, 1.0, perm)
        loc_ref[slot, c * DISPATCH_CHUNK:(c + 1) * DISPATCH_CHUNK, :] = jnp.dot(
            perm.astype(BF16), x, preferred_element_type=F32)

    def slot_copy(which):
        def copy(src_row, dst_row, rows=ROW_ALIGN):
            return pltpu.make_async_copy(loc_ref.at[which, pl.ds(src_row, rows)],
                                         xs_hbm.at[pl.ds(dst_row, rows)], sem.at[which])
        return copy

    def zero_copy(src_row, dst_row, rows=ROW_ALIGN):
        return pltpu.make_async_copy(zero_ref.at[pl.ds(src_row, rows)],
                                     xs_hbm.at[pl.ds(dst_row, rows)], zsem)

    def zero_block_copy(block):
        return pltpu.make_async_copy(
            zero_ref, xs_hbm.at[pl.ds(pl.multiple_of(block * MOE_ROWS, MOE_ROWS), MOE_ROWS)], zsem)

    _run_copies(nch_s, loff_s, gbase_s, i, slot_copy(slot))

    @pl.when(i > 0)
    def _():
        _drain(ntot_s[jnp.maximum(i - 1, 0)], slot_copy(1 - slot))

    @pl.when(i == pl.num_programs(0) - 1)
    def _():
        _drain(ntot_s[i], slot_copy(slot))
        zero_ref[...] = jnp.zeros_like(zero_ref)

        def per_expert(e, total):
            def per_chunk(c, carry):
                zero_copy(0, pl.multiple_of(pstart_s[e] + c * ROW_ALIGN, ROW_ALIGN)).start()
                return carry

            lax.fori_loop(0, pn_s[e], per_chunk, 0)
            return total + pn_s[e]

        total = lax.fori_loop(0, N_EXPERTS, per_expert, 0)
        _drain(total, zero_copy)
        n_blocks = xs_hbm.shape[0] // MOE_ROWS

        def start_block(blk, carry):
            zero_block_copy(blk).start()
            return carry

        def wait_block(blk, carry):
            zero_block_copy(blk).wait()
            return carry

        lax.fori_loop(nu_s[0], n_blocks, start_block, 0)
        lax.fori_loop(nu_s[0], n_blocks, wait_block, 0)


def _local_rows(tm):
    n_local = _round_up(TOP_K * tm + N_EXPERTS * (ROW_ALIGN - 1), 2 * LANES)
    assert n_local % DISPATCH_CHUNK == 0 and n_local % COMBINE_CHUNK == 0, n_local
    return n_local


def _dispatch(h, idx, plan, u, n_rows):
    t, d = h.shape
    tm = min(MOE_TILE, t)
    n_local = _local_rows(tm)
    grid_spec = pltpu.PrefetchScalarGridSpec(
        num_scalar_prefetch=7,
        grid=(t // tm,),
        in_specs=[
            pl.BlockSpec((tm, d), lambda i, *_: (i, 0)),
            pl.BlockSpec((SUBLANES, tm), lambda i, *_: (0, i)),
            pl.BlockSpec((1, ROUTER_PAD, 1), lambda i, *_: (i, 0, 0)),
            pl.BlockSpec((tm, tm), lambda i, *_: (0, 0)),
        ],
        out_specs=pl.BlockSpec(memory_space=pl.ANY),
        scratch_shapes=[
            pltpu.VMEM((2, n_local, d), F32),
            pltpu.VMEM((MOE_ROWS, d), F32),
            pltpu.SemaphoreType.DMA((2,)),
            pltpu.SemaphoreType.DMA,
        ],
    )
    return pl.pallas_call(
        _dispatch_kernel,
        grid_spec=grid_spec,
        out_shape=jax.ShapeDtypeStruct((n_rows, d), F32),
        compiler_params=_cparams("arbitrary"),
        name="dispatch",
    )(plan["nch"], plan["gbase"], plan["loff"], plan["ntot"], plan["pad_start"], plan["pad_n"],
      plan["n_used"], h, idx, plan["loffv"], u)


def _expert_kernel(be_ref, nu_ref, x_ref, wu_ref, bu_ref, wd_ref, bd_ref, o_ref, wu_bf, wd_bf):
    i = pl.program_id(0)

    @pl.when(jnp.logical_or(i == 0, be_ref[i] != be_ref[jnp.maximum(i - 1, 0)]))
    def _():
        wu_bf[...] = wu_ref[0, 0].astype(BF16)
        wd_bf[...] = wd_ref[0, 0].astype(BF16)

    @pl.when(i < nu_ref[0])
    def _():
        hdn = jnp.dot(x_ref[...].astype(BF16), wu_bf[...], preferred_element_type=F32) + bu_ref[0, 0]
        glu = jnp.minimum(hdn[:, :D_FF], SWIGLU_LIMIT)
        lin = jnp.clip(hdn[:, D_FF:], -SWIGLU_LIMIT, SWIGLU_LIMIT)
        act = glu * jax.nn.sigmoid(SWIGLU_ALPHA * glu) * (lin + 1.0)
        o_ref[...] = jnp.dot(act.astype(BF16), wd_bf[...], preferred_element_type=F32) + bd_ref[0, 0]

    @pl.when(i >= nu_ref[0])
    def _():
        o_ref[...] = jnp.zeros_like(o_ref)


def _experts(xs, plan, w_up, b_up, w_down, b_down, layer):
    n_rows, d = xs.shape
    depth, e = w_up.shape[:2]
    used = lambda i, nu: jnp.maximum(jnp.minimum(i, nu[0] - 1), 0)
    grid_spec = pltpu.PrefetchScalarGridSpec(
        num_scalar_prefetch=2,
        grid=(n_rows // MOE_ROWS,),
        in_specs=[
            pl.BlockSpec((MOE_ROWS, d), lambda i, be, nu: (used(i, nu), 0)),
            pl.BlockSpec((1, 1, d, 2 * D_FF), lambda i, be, nu: (layer, be[i], 0, 0)),
            pl.BlockSpec((1, 1, 1, 2 * D_FF), lambda i, be, nu: (layer, be[i], 0, 0)),
            pl.BlockSpec((1, 1, D_FF, d), lambda i, be, nu: (layer, be[i], 0, 0)),
            pl.BlockSpec((1, 1, 1, d), lambda i, be, nu: (layer, be[i], 0, 0)),
        ],
        out_specs=pl.BlockSpec((MOE_ROWS, d), lambda i, be, nu: (i, 0)),
        scratch_shapes=[pltpu.VMEM((d, 2 * D_FF), BF16), pltpu.VMEM((D_FF, d), BF16)],
    )
    return pl.pallas_call(
        _expert_kernel,
        grid_spec=grid_spec,
        out_shape=jax.ShapeDtypeStruct((n_rows, d), F32),
        compiler_params=pltpu.CompilerParams(dimension_semantics=("arbitrary",),
                                             vmem_limit_bytes=EXPERT_VMEM_LIMIT),
        name="experts",
    )(plan["block_e"], plan["n_used"], xs, w_up, b_up.reshape(depth, e, 1, 2 * D_FF), w_down,
      b_down.reshape(depth, e, 1, d))


def _combine_kernel(nch_s, gbase_s, loff_s, ntot_s,
                    h_ref, idx_ref, gate_ref, loffv_ref, u_ref, g_ref, b_ref, ys_hbm,
                    o_ref, loc_ref, sem):
    i = pl.program_id(0)
    n_local, tm = loc_ref.shape[1], h_ref.shape[0]
    slot = i % 2

    def slot_copy(which):
        def copy(dst_row, src_row, rows=ROW_ALIGN):
            return pltpu.make_async_copy(ys_hbm.at[pl.ds(src_row, rows)],
                                         loc_ref.at[which, pl.ds(dst_row, rows)], sem.at[which])
        return copy

    @pl.when(i == 0)
    def _():
        loc_ref[...] = jnp.zeros_like(loc_ref)
        _run_copies(nch_s, loff_s, gbase_s, 0, slot_copy(0))

    @pl.when(i + 1 < pl.num_programs(0))
    def _():
        _run_copies(nch_s, loff_s, gbase_s, i + 1, slot_copy(1 - slot))

    js = _local_positions(idx_ref[...], loffv_ref[0], u_ref[...])
    gates = gate_ref[...]
    row = lax.broadcasted_iota(jnp.int32, (LANES, tm), 0)
    stacked = jnp.zeros((LANES, tm), F32)
    for kk in range(TOP_K):
        stacked = jnp.where(row == kk, js[kk].astype(F32), stacked)
        stacked = jnp.where(row == TOP_K + kk, gates[kk:kk + 1, :], stacked)
    cols = stacked.T
    pos = [cols[:, kk:kk + 1].astype(jnp.int32) for kk in range(TOP_K)]
    gate = [cols[:, TOP_K + kk:TOP_K + kk + 1] for kk in range(TOP_K)]
    _drain(ntot_s[i], slot_copy(slot))
    ffn = jnp.zeros((tm, h_ref.shape[1]), F32)
    for c in range(n_local // COMBINE_CHUNK):
        rr = lax.broadcasted_iota(jnp.int32, (tm, COMBINE_CHUNK), 1) + c * COMBINE_CHUNK
        gmat = jnp.zeros((tm, COMBINE_CHUNK), F32)
        for kk in range(TOP_K):
            gmat = jnp.where(rr == pos[kk], gate[kk], gmat)
        rows = loc_ref[slot, c * COMBINE_CHUNK:(c + 1) * COMBINE_CHUNK, :]
        ffn = ffn + jnp.dot(gmat.astype(BF16), rows.astype(BF16), preferred_element_type=F32)
    o_ref[...] = _layer_norm(DN_ALPHA * h_ref[...] + ffn, g_ref[...], b_ref[...])


def _combine_ln(h, ys, idx, gate, plan, u, g, b):
    t, d = h.shape
    tm = min(MOE_TILE, t)
    n_local = _local_rows(tm)
    grid_spec = pltpu.PrefetchScalarGridSpec(
        num_scalar_prefetch=4,
        grid=(t // tm,),
        in_specs=[
            pl.BlockSpec((tm, d), lambda i, *_: (i, 0)),
            pl.BlockSpec((SUBLANES, tm), lambda i, *_: (0, i)),
            pl.BlockSpec((SUBLANES, tm), lambda i, *_: (0, i)),
            pl.BlockSpec((1, ROUTER_PAD, 1), lambda i, *_: (i, 0, 0)),
            pl.BlockSpec((tm, tm), lambda i, *_: (0, 0)),
            pl.BlockSpec((1, d), lambda i, *_: (0, 0)),
            pl.BlockSpec((1, d), lambda i, *_: (0, 0)),
            pl.BlockSpec(memory_space=pl.ANY),
        ],
        out_specs=pl.BlockSpec((tm, d), lambda i, *_: (i, 0)),
        scratch_shapes=[pltpu.VMEM((2, n_local, d), F32), pltpu.SemaphoreType.DMA((2,))],
    )
    return pl.pallas_call(
        _combine_kernel,
        grid_spec=grid_spec,
        out_shape=jax.ShapeDtypeStruct((t, d), F32),
        compiler_params=_cparams("arbitrary"),
        name="combine_ln",
    )(plan["nch"], plan["gbase"], plan["loff"], plan["ntot"],
      h, idx, gate, plan["loffv"], u, g.reshape(1, d), b.reshape(1, d), ys)


def _moe_ln(h, w_router, b_router, w_up, b_up, w_down, b_down, layer, ln_g, ln_b):
    t, d = h.shape
    tm = min(MOE_TILE, t)
    nt = t // tm
    n_rows = _round_up(t * TOP_K + nt * N_EXPERTS * (ROW_ALIGN - 1) + N_EXPERTS * (MOE_ROWS - 1),
                       MOE_ROWS)
    idx, gate, cnt = _router(h, w_router, b_router)
    plan = _moe_plan(cnt, n_rows)
    tri = jnp.arange(tm)
    u = (tri[:, None] < tri[None, :]).astype(BF16)
    xs = _dispatch(h, idx, plan, u, n_rows)
    ys = _experts(xs, plan, w_up, b_up, w_down, b_down, layer)
    return _combine_ln(h, ys, idx, gate, plan, u, ln_g, ln_b)


def _ret_rope_tables(seq):
    inv_freq = 1.0 / (RET_ROPE_THETA ** jnp.linspace(0.0, 1.0, RET_QK_DIM // 2, dtype=F32))
    ang = jnp.arange(seq, dtype=F32)[:, None] * inv_freq[None, :]
    cos, sin = jnp.cos(ang), jnp.sin(ang)
    return jnp.concatenate([cos, cos], -1), jnp.concatenate([-sin, sin], -1)


def _partial_rope_tables(seq):
    inv_freq = 1.0 / (ROPE_THETA ** (jnp.arange(0, ROPE_DIM, 2, dtype=F32) / ROPE_DIM))
    ang = jnp.arange(seq, dtype=F32)[:, None] * inv_freq[None, :]
    cos, sin = jnp.cos(ang), jnp.sin(ang)
    pad = jnp.zeros((seq, DIFF_HEAD_DIM - ROPE_DIM), F32)
    c64 = jnp.concatenate([cos, cos, pad + 1.0], -1)
    s64 = jnp.concatenate([-sin, sin, pad], -1)
    return jnp.concatenate([c64, c64], -1), jnp.concatenate([s64, s64], -1)


def kernel(x, ret_w_in, ret_w_out, kv_w, diff_w_q, diff_lambda, diff_subln_g, diff_w_out,
           ln_attn_g, ln_attn_b, ln_ffn_g, ln_ffn_b, moe_w_router, moe_b_router,
           moe_w_up, moe_b_up, moe_w_down, moe_b_down):
    batch, seq, d = x.shape
    t = batch * seq
    h = x.reshape(t, d)
    ret_cos, ret_sin = _ret_rope_tables(seq)
    par_cos, par_sin = _partial_rope_tables(seq)
    qk_scale = jnp.concatenate([jnp.ones((1, d), F32), jnp.full((1, d), RET_QK_DIM ** -0.5, F32)], -1)
    k_sh = v_sh = None
    for l in range(DEPTH):
        if l < N_A_LAYERS:
            w_in = ret_w_in[l].astype(BF16)
            qk = _proj(h, w_in[:, :2 * d], BF16, rope="half", cos=ret_cos, sin=ret_sin,
                       scale=qk_scale, seq=seq)
            v = _proj(h, w_in[:, 2 * d:2 * d + RET_V_WIDTH], BF16)
            g = _proj(h, w_in[:, 2 * d + RET_V_WIDTH:], F32)
            a = _retention(qk, v, g, batch, seq)
            h = _outproj_ln(a, ret_w_out[l].astype(BF16), h, ln_attn_g[l], ln_attn_b[l])
        else:
            j = l - N_A_LAYERS
            lambda_init = 0.8 - 0.6 * math.exp(-0.3 * l)
            q = _proj(h, diff_w_q[j].astype(BF16), BF16, rope="partial", cos=par_cos, sin=par_sin,
                      scale=jnp.full((1, d), DIFF_HEAD_DIM ** -0.5 * math.log2(math.e), F32), seq=seq)
            a = _diff_attention(q, k_sh, v_sh, diff_lambda[j], diff_subln_g[j], lambda_init, batch, seq)
            h = _outproj_ln(a, diff_w_out[j].astype(BF16), h, ln_attn_g[l], ln_attn_b[l])
        h = _moe_ln(h, moe_w_router[l], moe_b_router[l], moe_w_up, moe_b_up, moe_w_down, moe_b_down,
                    l, ln_ffn_g[l], ln_ffn_b[l])
        if l == N_A_LAYERS - 1:
            kvw = kv_w.astype(BF16)
            k_sh = _proj(h, kvw[:, :d], BF16, rope="partial", cos=par_cos, sin=par_sin, seq=seq)
            v_sh = _transpose_values(_proj(h, kvw[:, d:], BF16), batch, seq)
    return h.reshape(batch, seq, d)
```
